```python
import jax, jax.numpy as jnp
from jax import lax
import numpy as np

D_MODEL = 2048
BATCH = 4
SEQ = 2048
DEPTH = 4
DEC_BATCH = 8
DEC_SEQ = 4
PAST_LEN = 16384
PAGE_SIZE = 128

N_EVEN = (DEPTH + 1) // 2
N_ODD = DEPTH // 2
CONV_DIM = D_MODEL // 2
CONV_GROUPS = 16
CONV_W = 3
SG_DIM = D_MODEL // 2
SG_HEADS = 8
SG_HEAD_DIM = SG_DIM // SG_HEADS
CHUNK = 128
SB_HEADS = 16
SB_HEAD_DIM = D_MODEL // SB_HEADS
SB_BLOCK = 128
SB_BIAS_MAX = 12.0
D_FF = -(-8 * D_MODEL // (3 * 256)) * 256
IN_AB = 3 * CONV_DIM + 2 * SG_DIM
EPS = 1e-6

kernel_name = "hybrid_shortconv_gmlp_stickbreak_step"


def rms_norm(x, g):
    x32 = x.astype(jnp.float32)
    y = x32 * lax.rsqrt(jnp.mean(x32 * x32, axis=-1, keepdims=True) + EPS)
    return (y * g.astype(jnp.float32)).astype(x.dtype)


def layer_norm(x, g, b):
    x32 = x.astype(jnp.float32)
    xc = x32 - jnp.mean(x32, axis=-1, keepdims=True)
    y = xc * lax.rsqrt(jnp.mean(xc * xc, axis=-1, keepdims=True) + EPS)
    return (y * g.astype(jnp.float32) + b.astype(jnp.float32)).astype(x.dtype)


def short_conv(u, prefix, w):
    L = u.shape[1]
    full = jnp.concatenate([prefix.astype(u.dtype), u], axis=1)
    out = w[0] * full[:, 0:L]
    for j in range(1, CONV_W):
        out = out + w[j] * full[:, j:j + L]
    return out, full[:, L:]


def spatial_gate(vn, sg_w, sg_b):
    Bn, L, _ = vn.shape
    nc = -(-L // CHUNK)
    pad = nc * CHUNK - L
    vp = jnp.pad(vn, ((0, 0), (0, pad), (0, 0))).reshape(Bn, nc, CHUNK, SG_HEADS, SG_HEAD_DIM)
    w = jnp.tril(sg_w)
    s = jnp.einsum('hts,bcshd->bcthd', w, vp) + sg_b.T[None, None, :, :, None]
    return s.reshape(Bn, nc * CHUNK, SG_DIM)[:, :L]


def stick_breaking_attention(q, k, v, bias, q_pos, k_pos):
    Bn, Lq, H, Dh = q.shape
    qb = min(SB_BLOCK, Lq)
    nb = -(-Lq // qb)
    pad = nb * qb - Lq
    qs = jnp.pad(q, ((0, 0), (0, pad), (0, 0), (0, 0))).reshape(Bn, nb, qb, H, Dh).swapaxes(0, 1)
    ps = jnp.pad(q_pos, (0, pad)).reshape(nb, qb)
    scale = Dh ** -0.5
    b32 = bias.astype(jnp.float32)[None, :, None, None]

    def block(args):
        qblk, pblk = args
        z = jnp.einsum('bqhd,bkhd->bhqk', qblk, k).astype(jnp.float32) * scale + b32
        mask = k_pos[None, :] < pblk[:, None]
        log_keep = jnp.where(mask, jax.nn.log_sigmoid(-z), 0.0)
        after = lax.cumsum(log_keep, axis=3, reverse=True) - log_keep
        a = jnp.where(mask, jnp.exp(jax.nn.log_sigmoid(z) + after), 0.0)
        return jnp.einsum('bhqk,bkhd->bqhd', a.astype(v.dtype), v)

    o = lax.map(block, (qs, ps))
    return o.swapaxes(0, 1).reshape(Bn, nb * qb, H, Dh)[:, :Lq]


def even_mixer(h, prefix, w_in, conv_w, ln_g, ln_b, sg_w, sg_b, w_out):
    proj = h @ w_in
    gate_b = proj[..., 0:CONV_DIM]
    gate_c = proj[..., CONV_DIM:2 * CONV_DIM]
    xa = proj[..., 2 * CONV_DIM:3 * CONV_DIM]
    u = proj[..., 3 * CONV_DIM:3 * CONV_DIM + SG_DIM]
    vv = proj[..., 3 * CONV_DIM + SG_DIM:]
    conv_out, new_prefix = short_conv(gate_c * xa, prefix, conv_w)
    y_a = gate_b * conv_out
    vn = layer_norm(vv, ln_g, ln_b)
    y_b = u * spatial_gate(vn, sg_w, sg_b)
    L = h.shape[1]
    start = ((L - 1) // CHUNK) * CHUNK
    y = jnp.concatenate([y_a, y_b], axis=-1) @ w_out
    return y, new_prefix, vn[:, start:]


def odd_mixer(h, past_k, past_v, w_qkv, sb_bias, w_o):
    Bn, L, _ = h.shape
    qkv = (h @ w_qkv).reshape(Bn, L, 3, SB_HEADS, SB_HEAD_DIM)
    q, k, v = qkv[:, :, 0], qkv[:, :, 1], qkv[:, :, 2]
    P = past_k.shape[1]
    k_all = jnp.concatenate([past_k.astype(k.dtype), k], axis=1)
    v_all = jnp.concatenate([past_v.astype(v.dtype), v], axis=1)
    q_pos = jnp.arange(L, dtype=jnp.int32) + P
    k_pos = jnp.arange(P + L, dtype=jnp.int32)
    o = stick_breaking_attention(q, k_all, v_all, sb_bias, q_pos, k_pos)
    return o.reshape(Bn, L, SB_HEADS * SB_HEAD_DIM) @ w_o, k, v


def swiglu(h, wg, wu, wd):
    return (jax.nn.silu(h @ wg) * (h @ wu)) @ wd


def trunk(x, conv_state, get_past, p):
    conv_new, sgv_new, k_new, v_new = [], [], [], []
    for l in range(DEPTH):
        i = l // 2
        h = rms_norm(x, p['norm_mix_pre'][l])
        if l % 2 == 0:
            m, c, vr = even_mixer(h, conv_state[i], p['w_in_ab'][i], p['conv_w'][i], p['sg_ln_g'][i],
                                  p['sg_ln_b'][i], p['sg_w'][i], p['sg_b'][i], p['w_out_ab'][i])
            conv_new.append(c)
            sgv_new.append(vr)
        else:
            pk, pv = get_past(i)
            m, kr, vr = odd_mixer(h, pk, pv, p['w_qkv'][i], p['sb_bias'][i], p['w_o'][i])
            k_new.append(kr)
            v_new.append(vr)
        x = x + rms_norm(m, p['norm_mix_post'][l])
        h = rms_norm(x, p['norm_ffn_pre'][l])
        x = x + rms_norm(swiglu(h, p['w_gate'][l], p['w_up'][l], p['w_down'][l]), p['norm_ffn_post'][l])
    return x, jnp.stack(conv_new), jnp.stack(sgv_new), jnp.stack(k_new), jnp.stack(v_new)


def setup_inputs(seed: int = 0) -> dict:
    key = jax.random.key(seed)
    ks = jax.random.split(key, 24)
    n_pages = PAST_LEN // PAGE_SIZE
    n_pool = (DEC_BATCH * n_pages * 5) // 4
    nrm = jax.random.normal

    def w(k, shape, fan_in):
        return nrm(k, shape, jnp.float32) * (fan_in ** -0.5)

    def gain(k, shape):
        return 1.0 + 0.05 * nrm(k, shape, jnp.float32)

    page_table = jax.random.permutation(ks[5], n_pool)[:DEC_BATCH * n_pages]
    page_table = page_table.reshape(DEC_BATCH, n_pages).astype(jnp.int32)
    sb_bias = (-jnp.linspace(0.0, SB_BIAS_MAX, SB_HEADS, dtype=jnp.float32)[None, :]
               + 0.1 * nrm(ks[22], (N_ODD, SB_HEADS), jnp.float32))
    return {
        'x_prompt': nrm(ks[0], (BATCH, SEQ, D_MODEL), jnp.float32),
        'x_sample': nrm(ks[1], (DEC_BATCH, DEC_SEQ, D_MODEL), jnp.float32),
        'state_conv': nrm(ks[2], (N_EVEN, DEC_BATCH, CONV_W - 1, CONV_DIM), jnp.float32),
        'cache_k': nrm(ks[3], (N_ODD, n_pool, PAGE_SIZE, SB_HEADS, SB_HEAD_DIM), jnp.float32),
        'cache_v': nrm(ks[4], (N_ODD, n_pool, PAGE_SIZE, SB_HEADS, SB_HEAD_DIM), jnp.float32),
        'page_table': page_table,
        'norm_mix_pre': gain(ks[6], (DEPTH, D_MODEL)),
        'norm_mix_post': gain(ks[7], (DEPTH, D_MODEL)),
        'norm_ffn_pre': gain(ks[8], (DEPTH, D_MODEL)),
        'norm_ffn_post': gain(ks[9], (DEPTH, D_MODEL)),
        'w_in_ab': w(ks[10], (N_EVEN, D_MODEL, IN_AB), D_MODEL),
        'conv_w': w(ks[11], (N_EVEN, CONV_W, CONV_DIM), CONV_W),
        'sg_ln_g': gain(ks[12], (N_EVEN, SG_DIM)),
        'sg_ln_b': 0.01 * nrm(ks[13], (N_EVEN, SG_DIM), jnp.float32),
        'sg_w': w(ks[14], (N_EVEN, SG_HEADS, CHUNK, CHUNK), CHUNK),
        'sg_b': 1.0 + 0.01 * nrm(ks[15], (N_EVEN, SG_HEADS, CHUNK), jnp.float32),
        'w_out_ab': w(ks[16], (N_EVEN, CONV_DIM + SG_DIM, D_MODEL), CONV_DIM + SG_DIM),
        'w_qkv': w(ks[17], (N_ODD, D_MODEL, 3 * SB_HEADS * SB_HEAD_DIM), D_MODEL),
        'sb_bias': sb_bias,
        'w_o': w(ks[18], (N_ODD, SB_HEADS * SB_HEAD_DIM, D_MODEL), SB_HEADS * SB_HEAD_DIM),
        'w_gate': w(ks[19], (DEPTH, D_MODEL, D_FF), D_MODEL),
        'w_up': w(ks[20], (DEPTH, D_MODEL, D_FF), D_MODEL),
        'w_down': w(ks[21], (DEPTH, D_FF, D_MODEL), D_FF),
    }


def reference(x_prompt, x_sample, state_conv, cache_k, cache_v, page_table,
              norm_mix_pre, norm_mix_post, norm_ffn_pre, norm_ffn_post,
              w_in_ab, conv_w, sg_ln_g, sg_ln_b, sg_w, sg_b, w_out_ab,
              w_qkv, sb_bias, w_o, w_gate, w_up, w_down):
    p = dict(norm_mix_pre=norm_mix_pre, norm_mix_post=norm_mix_post, norm_ffn_pre=norm_ffn_pre,
             norm_ffn_post=norm_ffn_post, w_in_ab=w_in_ab, conv_w=conv_w, sg_ln_g=sg_ln_g,
             sg_ln_b=sg_ln_b, sg_w=sg_w, sg_b=sg_b, w_out_ab=w_out_ab, w_qkv=w_qkv, sb_bias=sb_bias,
             w_o=w_o, w_gate=w_gate, w_up=w_up, w_down=w_down)

    bp = x_prompt.shape[0]
    conv0 = jnp.zeros((N_EVEN, bp, CONV_W - 1, CONV_DIM), x_prompt.dtype)
    empty = jnp.zeros((bp, 0, SB_HEADS, SB_HEAD_DIM), x_prompt.dtype)

    def prompt_past(i):
        return empty, empty

    y_prompt, conv_p, sgv_p, k_p, v_p = trunk(x_prompt, conv0, prompt_past, p)

    bs, n_pages = page_table.shape

    def sample_past(i):
        pk = cache_k[i][page_table].reshape(bs, n_pages * PAGE_SIZE, SB_HEADS, SB_HEAD_DIM)
        pv = cache_v[i][page_table].reshape(bs, n_pages * PAGE_SIZE, SB_HEADS, SB_HEAD_DIM)
        return pk, pv

    y_sample, conv_s, sgv_s, k_s, v_s = trunk(x_sample, state_conv, sample_past, p)
    return (y_prompt, y_sample, conv_p, conv_s, sgv_p, sgv_s, k_p, v_p, k_s, v_s)
```

```python
import functools

import jax
import jax.numpy as jnp
from jax import lax
from jax.experimental import pallas as pl
from jax.experimental.pallas import tpu as pltpu

D_MODEL = 2048
DEPTH = 4
PAGE_SIZE = 128
CONV_DIM = D_MODEL // 2
CONV_W = 3
SG_DIM = D_MODEL // 2
SG_HEADS = 8
SG_HEAD_DIM = SG_DIM // SG_HEADS
CHUNK = 128
SB_HEADS = 16
SB_HEAD_DIM = D_MODEL // SB_HEADS
D_FF = 5632
IN_AB = 3 * CONV_DIM + 2 * SG_DIM
EPS = 1e-6

V7X_LANES = 128
V7X_VMEM_LIMIT_BYTES = 56 * 1024 * 1024

BF16 = jnp.bfloat16
F32 = jnp.float32


def _cparams(n_axes):
    return pltpu.CompilerParams(
        dimension_semantics=("arbitrary",) * n_axes,
        vmem_limit_bytes=V7X_VMEM_LIMIT_BYTES,
    )


def _row_block(m, want):
    return want if m % want == 0 else m


def _rms_cast_kernel(x_ref, g_ref, h_ref):
    x = x_ref[...]
    y = x * lax.rsqrt(jnp.mean(x * x, axis=-1, keepdims=True) + EPS)
    h_ref[...] = (y * g_ref[...]).astype(h_ref.dtype)


def rms_cast(x, g):
    m, d = x.shape
    bm = _row_block(m, 512)
    return pl.pallas_call(
        _rms_cast_kernel,
        out_shape=jax.ShapeDtypeStruct((m, d), BF16),
        grid=(m // bm,),
        in_specs=[pl.BlockSpec((bm, d), lambda i: (i, 0)),
                  pl.BlockSpec((1, d), lambda i: (0, 0))],
        out_specs=pl.BlockSpec((bm, d), lambda i: (i, 0)),
        compiler_params=_cparams(1),
        name="rms_cast",
    )(x, g.reshape(1, d))


def _post_norm_kernel(x_ref, m_ref, gp_ref, gn_ref, xo_ref, h_ref):
    m = m_ref[...].astype(F32)
    y = m * lax.rsqrt(jnp.mean(m * m, axis=-1, keepdims=True) + EPS)
    xn = x_ref[...] + y * gp_ref[...]
    xo_ref[...] = xn
    hn = xn * lax.rsqrt(jnp.mean(xn * xn, axis=-1, keepdims=True) + EPS)
    h_ref[...] = (hn * gn_ref[...]).astype(h_ref.dtype)


def post_norm(x, m, g_post, g_next):
    rows, d = x.shape
    bm = _row_block(rows, 512)
    return pl.pallas_call(
        _post_norm_kernel,
        out_shape=(jax.ShapeDtypeStruct((rows, d), F32), jax.ShapeDtypeStruct((rows, d), BF16)),
        grid=(rows // bm,),
        in_specs=[pl.BlockSpec((bm, d), lambda i: (i, 0)),
                  pl.BlockSpec((bm, d), lambda i: (i, 0)),
                  pl.BlockSpec((1, d), lambda i: (0, 0)),
                  pl.BlockSpec((1, d), lambda i: (0, 0))],
        out_specs=(pl.BlockSpec((bm, d), lambda i: (i, 0)),
                   pl.BlockSpec((bm, d), lambda i: (i, 0))),
        compiler_params=_cparams(1),
        name="post_norm",
    )(x, m, g_post.reshape(1, d), g_next.reshape(1, d))


def _mm_kernel(x_ref, w_ref, o_ref, wb_ref):
    @pl.when(pl.program_id(1) == 0)
    def _():
        wb_ref[...] = w_ref[...].astype(BF16)

    o_ref[...] = jnp.dot(x_ref[...].astype(BF16), wb_ref[...],
                         preferred_element_type=F32).astype(o_ref.dtype)


def matmul(x, w, layer, *, bm, bn, out_dtype=F32):
    m, k = x.shape
    n = w.shape[2]
    bm = _row_block(m, bm)
    return pl.pallas_call(
        _mm_kernel,
        out_shape=jax.ShapeDtypeStruct((m, n), out_dtype),
        grid=(n // bn, m // bm),
        in_specs=[pl.BlockSpec((bm, k), lambda j, i: (i, 0)),
                  pl.BlockSpec((None, k, bn), lambda j, i: (layer, 0, j))],
        out_specs=pl.BlockSpec((bm, bn), lambda j, i: (i, j)),
        scratch_shapes=[pltpu.VMEM((k, bn), BF16)],
        compiler_params=_cparams(2),
        name="matmul",
    )(x, w)


def _swiglu_up_kernel(x_ref, wg_ref, wu_ref, o_ref, wgb_ref, wub_ref):
    @pl.when(pl.program_id(1) == 0)
    def _():
        wgb_ref[...] = wg_ref[...].astype(BF16)
        wub_ref[...] = wu_ref[...].astype(BF16)

    x = x_ref[...]
    g = jnp.dot(x, wgb_ref[...], preferred_element_type=F32)
    u = jnp.dot(x, wub_ref[...], preferred_element_type=F32)
    o_ref[...] = (g * jax.nn.sigmoid(g) * u).astype(o_ref.dtype)


def swiglu_up(x, wg, wu, layer, *, bm, bn):
    m, k = x.shape
    n = wg.shape[2]
    bm = _row_block(m, bm)
    return pl.pallas_call(
        _swiglu_up_kernel,
        out_shape=jax.ShapeDtypeStruct((m, n), BF16),
        grid=(n // bn, m // bm),
        in_specs=[pl.BlockSpec((bm, k), lambda j, i: (i, 0)),
                  pl.BlockSpec((None, k, bn), lambda j, i: (layer, 0, j)),
                  pl.BlockSpec((None, k, bn), lambda j, i: (layer, 0, j))],
        out_specs=pl.BlockSpec((bm, bn), lambda j, i: (i, j)),
        scratch_shapes=[pltpu.VMEM((k, bn), BF16), pltpu.VMEM((k, bn), BF16)],
        compiler_params=_cparams(2),
        name="swiglu_up",
    )(x, wg, wu)


def _even_mix_kernel(gb_ref, gc_ref, xa_ref, u_ref, vv_ref, prefix_ref, cw_ref, lng_ref, lnb_ref,
                     sgw_ref, sgbt_ref, y_ref, cnew_ref, vlast_ref, cbuf_ref, wtri_ref, pad_ref,
                     *, rows, n_steps):
    b = pl.program_id(0)
    l = pl.program_id(1)
    carry_at = 8 - (CONV_W - 1)

    @pl.when((b == 0) & (l == 0))
    def _():
        t = lax.broadcasted_iota(jnp.int32, (CHUNK, CHUNK), 0)
        s = lax.broadcasted_iota(jnp.int32, (CHUNK, CHUNK), 1)
        for h in range(SG_HEADS):
            wtri_ref[h] = jnp.where(s <= t, sgw_ref[h], 0.0).astype(BF16)

    @pl.when(l == 0)
    def _():
        cbuf_ref[carry_at:8, :] = prefix_ref[0]

    c = gc_ref[0] * xa_ref[0]
    cbuf_ref[8:8 + rows, :] = c
    conv = cw_ref[CONV_W - 1:CONV_W, :] * c
    for j in range(CONV_W - 1):
        conv = conv + cw_ref[j:j + 1, :] * cbuf_ref[carry_at + j:carry_at + j + rows, :]
    y_ref[0, :, 0:CONV_DIM] = (gb_ref[0] * conv).astype(y_ref.dtype)
    new_hist = cbuf_ref[8 + rows - (CONV_W - 1):8 + rows, :]
    cbuf_ref[carry_at:8, :] = new_hist

    @pl.when(l == n_steps - 1)
    def _():
        cnew_ref[0] = new_hist

    vv = vv_ref[0]
    mu = jnp.mean(vv, axis=-1, keepdims=True)
    xc = vv - mu
    vn = xc * lax.rsqrt(jnp.mean(xc * xc, axis=-1, keepdims=True) + EPS) * lng_ref[...] + lnb_ref[...]

    n_chunks = -(-rows // CHUNK)
    last_rows = rows - (n_chunks - 1) * CHUNK

    @pl.when(l == n_steps - 1)
    def _():
        vlast_ref[0] = vn[(n_chunks - 1) * CHUNK:, :]

    if rows % CHUNK != 0:
        pad_ref[...] = jnp.zeros_like(pad_ref)
        pad_ref[0:rows, :] = vn
        vn_full = pad_ref[...]
    else:
        vn_full = vn
    vnb = vn_full.astype(BF16)
    u = u_ref[0]
    for ci in range(n_chunks):
        r0 = ci * CHUNK
        nr = CHUNK if ci < n_chunks - 1 else last_rows
        for h in range(SG_HEADS):
            c0 = h * SG_HEAD_DIM
            s = jnp.dot(wtri_ref[h], vnb[r0:r0 + CHUNK, c0:c0 + SG_HEAD_DIM], preferred_element_type=F32)
            s = s + sgbt_ref[:, h:h + 1]
            yb = u[r0:r0 + nr, c0:c0 + SG_HEAD_DIM] * s[0:nr]
            y_ref[0, r0:r0 + nr, CONV_DIM + c0:CONV_DIM + c0 + SG_HEAD_DIM] = yb.astype(y_ref.dtype)


def even_mix(proj, prefix, conv_w, ln_g, ln_b, sg_w, sg_b, *, rows):
    bn, seq, _ = proj.shape
    rows = min(rows, seq)
    n_steps = seq // rows
    r_last = seq - ((seq - 1) // CHUNK) * CHUNK
    pad_rows = -(-rows // CHUNK) * CHUNK

    def col(cb):
        return pl.BlockSpec((1, rows, CONV_DIM), lambda b, l: (b, l, cb))

    const2 = lambda b, l: (0, 0)
    y_dtype = BF16 if rows % 16 == 0 else F32
    return pl.pallas_call(
        functools.partial(_even_mix_kernel, rows=rows, n_steps=n_steps),
        out_shape=(jax.ShapeDtypeStruct((bn, seq, D_MODEL), y_dtype),
                   jax.ShapeDtypeStruct((bn, CONV_W - 1, CONV_DIM), F32),
                   jax.ShapeDtypeStruct((bn, r_last, SG_DIM), F32)),
        grid=(bn, n_steps),
        in_specs=[col(0), col(1), col(2), col(3), col(4),
                  pl.BlockSpec((1, CONV_W - 1, CONV_DIM), lambda b, l: (b, 0, 0)),
                  pl.BlockSpec((CONV_W, CONV_DIM), const2),
                  pl.BlockSpec((1, SG_DIM), const2),
                  pl.BlockSpec((1, SG_DIM), const2),
                  pl.BlockSpec((SG_HEADS, CHUNK, CHUNK), lambda b, l: (0, 0, 0)),
                  pl.BlockSpec((CHUNK, SG_HEADS), const2)],
        out_specs=(pl.BlockSpec((1, rows, D_MODEL), lambda b, l: (b, l, 0)),
                   pl.BlockSpec((1, CONV_W - 1, CONV_DIM), lambda b, l: (b, 0, 0)),
                   pl.BlockSpec((1, r_last, SG_DIM), lambda b, l: (b, 0, 0))),
        scratch_shapes=[pltpu.VMEM((8 + rows, CONV_DIM), F32),
                        pltpu.VMEM((SG_HEADS, CHUNK, CHUNK), BF16),
                        pltpu.VMEM((pad_rows, SG_DIM), F32)],
        compiler_params=_cparams(2),
        name="even_mix",
    )(proj, proj, proj, proj, proj, prefix, conv_w, ln_g.reshape(1, SG_DIM), ln_b.reshape(1, SG_DIM),
      sg_w, sg_b.T)


def _sb_terms(z, visible):
    l1p = jnp.log1p(jnp.exp(-jnp.abs(z)))
    log_keep = -(jnp.maximum(z, 0.0) + l1p)
    log_beta = jnp.minimum(z, 0.0) - l1p
    if visible is not None:
        log_keep = jnp.where(visible, log_keep, 0.0)
    return log_keep, log_beta


def _split_bf16(x):
    hi = x.astype(BF16)
    lo = (x - hi.astype(F32)).astype(BF16)
    return hi, lo


def _sb_prompt_kernel(bias_ref, q_ref, k_ref, v_ref, o_ref, kb_ref, vb_ref, *, tq):
    h = pl.program_id(1)
    i = pl.program_id(2)

    @pl.when(i == 0)
    def _():
        kb_ref[...] = k_ref[0].astype(BF16)
        vb_ref[...] = v_ref[0].astype(BF16)

    q = q_ref[0].astype(BF16)
    bias = bias_ref[h]
    scale = SB_HEAD_DIM ** -0.5
    row = lax.broadcasted_iota(jnp.int32, (tq, tq), 0)
    colm = lax.broadcasted_iota(jnp.int32, (tq, tq), 1)
    later = jnp.where(row > colm, 1.0, 0.0).astype(BF16)

    def body(step, carry):
        acc, run = carry
        j = i - step
        start = pl.multiple_of(j * tq, tq)
        ks = kb_ref[pl.ds(start, tq), :]
        vs = vb_ref[pl.ds(start, tq), :]
        z = lax.dot_general(q, ks, (((1,), (1,)), ((), ())), preferred_element_type=F32) * scale + bias
        visible = (colm + j * tq) < (row + i * tq)
        log_keep, log_beta = _sb_terms(z, visible)
        hi, lo = _split_bf16(log_keep)
        after = (jnp.dot(hi, later, preferred_element_type=F32)
                 + jnp.dot(lo, later, preferred_element_type=F32)) + run
        p = jnp.where(visible, jnp.exp(log_beta + after), 0.0)
        acc = acc + jnp.dot(p.astype(BF16), vs, preferred_element_type=F32)
        run = run + jnp.sum(log_keep, axis=1, keepdims=True)
        return acc, run

    acc, _ = lax.fori_loop(0, i + 1, body,
                           (jnp.zeros((tq, SB_HEAD_DIM), F32), jnp.zeros((tq, 1), F32)))
    o_ref[0] = acc.astype(o_ref.dtype)


def sb_attn_prompt(qkv, bias, *, tq):
    bn, seq, _ = qkv.shape
    nq = seq // tq
    return pl.pallas_call(
        functools.partial(_sb_prompt_kernel, tq=tq),
        out_shape=jax.ShapeDtypeStruct((bn, seq, D_MODEL), BF16),
        grid=(bn, SB_HEADS, nq),
        in_specs=[pl.BlockSpec(memory_space=pltpu.SMEM),
                  pl.BlockSpec((1, tq, SB_HEAD_DIM), lambda b, h, i: (b, i, h)),
                  pl.BlockSpec((1, seq, SB_HEAD_DIM), lambda b, h, i: (b, 0, SB_HEADS + h)),
                  pl.BlockSpec((1, seq, SB_HEAD_DIM), lambda b, h, i: (b, 0, 2 * SB_HEADS + h))],
        out_specs=pl.BlockSpec((1, tq, SB_HEAD_DIM), lambda b, h, i: (b, i, h)),
        scratch_shapes=[pltpu.VMEM((seq, SB_HEAD_DIM), BF16), pltpu.VMEM((seq, SB_HEAD_DIM), BF16)],
        compiler_params=_cparams(3),
        name="sb_attn_prompt",
    )(bias, qkv, qkv, qkv)


def _sb_sample_kernel(pt_ref, qbd_ref, bias_ref, knew_ref, vnew_ref, kpg_ref, vpg_ref, o_ref,
                      acc_ref, run_ref, tile_ref, *, n_new):
    s = pl.program_id(1)
    n_steps = pl.num_programs(1)
    scale = SB_HEAD_DIM ** -0.5
    key = lax.broadcasted_iota(jnp.int32, (PAGE_SIZE, V7X_LANES), 0)
    lane = lax.broadcasted_iota(jnp.int32, (PAGE_SIZE, V7X_LANES), 1)
    r = lax.broadcasted_iota(jnp.int32, (PAGE_SIZE, PAGE_SIZE), 0)
    c = lax.broadcasted_iota(jnp.int32, (PAGE_SIZE, PAGE_SIZE), 1)
    later_t = jnp.where(c > r, 1.0, 0.0).astype(BF16)

    def absorb(kt, vt, visible):
        z = jnp.dot(kt, qbd_ref[0], preferred_element_type=F32) * scale + bias_ref[...]
        log_keep, log_beta = _sb_terms(z, visible)
        hi, lo = _split_bf16(log_keep)
        after = (jnp.dot(later_t, hi, preferred_element_type=F32)
                 + jnp.dot(later_t, lo, preferred_element_type=F32)) + run_ref[...]
        p = jnp.exp(log_beta + after)
        if visible is not None:
            p = jnp.where(visible, p, 0.0)
        acc_ref[...] += lax.dot_general(p.astype(BF16), vt, (((0,), (0,)), ((), ())),
                                        preferred_element_type=F32)
        run_ref[...] += jnp.sum(log_keep, axis=0, keepdims=True)

    @pl.when(s == 0)
    def _():
        acc_ref[...] = jnp.zeros_like(acc_ref)
        run_ref[...] = jnp.zeros_like(run_ref)
        tile_ref[...] = jnp.zeros_like(tile_ref)
        tile_ref[0, 0:n_new, :] = knew_ref[0]
        tile_ref[1, 0:n_new, :] = vnew_ref[0]
        visible = (key < lane // SB_HEADS) & (key < n_new)
        absorb(tile_ref[0].astype(BF16), tile_ref[1].astype(BF16), visible)

    @pl.when(s > 0)
    def _():
        absorb(kpg_ref[0, 0].astype(BF16), vpg_ref[0, 0].astype(BF16), None)

    @pl.when(s == n_steps - 1)
    def _():
        rr = lax.broadcasted_iota(jnp.int32, (SB_HEADS, D_MODEL), 0)
        cc = lax.broadcasted_iota(jnp.int32, (SB_HEADS, D_MODEL), 1)
        own = (cc // SB_HEAD_DIM) == rr
        for i in range(n_new):
            blk = acc_ref[i * SB_HEADS:(i + 1) * SB_HEADS, :]
            o_ref[0, i:i + 1, :] = jnp.sum(jnp.where(own, blk, 0.0), axis=0, keepdims=True).astype(o_ref.dtype)


def sb_attn_sample(q, k_new, v_new, cache_k, cache_v, page_table, bias, layer):
    bn, n_new, _ = q.shape
    n_pages = page_table.shape[1]
    q4 = q.reshape(bn, n_new, SB_HEADS, SB_HEAD_DIM)
    eye = jnp.eye(SB_HEADS, dtype=q.dtype)
    qbd = jnp.einsum('bihd,hg->bhdig', q4, eye).reshape(bn, D_MODEL, n_new * SB_HEADS)
    qbd = jnp.pad(qbd, ((0, 0), (0, 0), (0, V7X_LANES - n_new * SB_HEADS))).astype(BF16)
    bias_l = jnp.pad(jnp.tile(bias, n_new), (0, V7X_LANES - n_new * SB_HEADS)).reshape(1, V7X_LANES)

    def page(b, s, pt):
        return (layer, pt[b, n_pages - jnp.maximum(s, 1)], 0, 0)

    grid_spec = pltpu.PrefetchScalarGridSpec(
        num_scalar_prefetch=1,
        grid=(bn, n_pages + 1),
        in_specs=[pl.BlockSpec((1, D_MODEL, V7X_LANES), lambda b, s, pt: (b, 0, 0)),
                  pl.BlockSpec((1, V7X_LANES), lambda b, s, pt: (0, 0)),
                  pl.BlockSpec((1, n_new, D_MODEL), lambda b, s, pt: (b, 0, 0)),
                  pl.BlockSpec((1, n_new, D_MODEL), lambda b, s, pt: (b, 0, 0)),
                  pl.BlockSpec((1, 1, PAGE_SIZE, D_MODEL), page),
                  pl.BlockSpec((1, 1, PAGE_SIZE, D_MODEL), page)],
        out_specs=pl.BlockSpec((1, n_new, D_MODEL), lambda b, s, pt: (b, 0, 0)),
        scratch_shapes=[pltpu.VMEM((V7X_LANES, D_MODEL), F32),
                        pltpu.VMEM((1, V7X_LANES), F32),
                        pltpu.VMEM((2, PAGE_SIZE, D_MODEL), F32)],
    )
    return pl.pallas_call(
        functools.partial(_sb_sample_kernel, n_new=n_new),
        out_shape=jax.ShapeDtypeStruct((bn, n_new, D_MODEL), F32),
        grid_spec=grid_spec,
        compiler_params=_cparams(2),
        name="sb_attn_sample",
    )(page_table, qbd, bias_l, k_new, v_new, cache_k, cache_v)


def _trunk(x3, conv_state, attend, p, *, bm, even_rows):
    bn, seq, d = x3.shape
    rows = bn * seq
    x = x3.reshape(rows, d)
    h = rms_cast(x, p['norm_mix_pre'][0])
    conv_new, sgv_new, k_new, v_new = [], [], [], []
    for l in range(DEPTH):
        i = l // 2
        if l % 2 == 0:
            proj = matmul(h, p['w_in_ab'], i, bm=bm, bn=1024)
            y, c, vr = even_mix(proj.reshape(bn, seq, IN_AB), conv_state[i], p['conv_w'][i], p['sg_ln_g'][i],
                                p['sg_ln_b'][i], p['sg_w'][i], p['sg_b'][i], rows=even_rows)
            conv_new.append(c)
            sgv_new.append(vr)
            m = matmul(y.reshape(rows, d), p['w_out_ab'], i, bm=bm, bn=1024)
        else:
            qkv = matmul(h, p['w_qkv'], i, bm=bm, bn=1024).reshape(bn, seq, 3 * d)
            o = attend(i, qkv)
            k_new.append(qkv[:, :, d:2 * d].reshape(bn, seq, SB_HEADS, SB_HEAD_DIM))
            v_new.append(qkv[:, :, 2 * d:].reshape(bn, seq, SB_HEADS, SB_HEAD_DIM))
            m = matmul(o.reshape(rows, d), p['w_o'], i, bm=bm, bn=1024)
        x, h = post_norm(x, m, p['norm_mix_post'][l], p['norm_ffn_pre'][l])
        a = swiglu_up(h, p['w_gate'], p['w_up'], l, bm=bm, bn=512)
        m = matmul(a, p['w_down'], l, bm=min(bm, 512), bn=512)
        g_next = p['norm_mix_pre'][l + 1] if l + 1 < DEPTH else p['norm_mix_pre'][0]
        x, h = post_norm(x, m, p['norm_ffn_post'][l], g_next)
    return (x.reshape(bn, seq, d), jnp.stack(conv_new), jnp.stack(sgv_new), jnp.stack(k_new), jnp.stack(v_new))


def kernel(x_prompt, x_sample, state_conv, cache_k, cache_v, page_table, norm_mix_pre, norm_mix_post,
           norm_ffn_pre, norm_ffn_post, w_in_ab, conv_w, sg_ln_g, sg_ln_b, sg_w, sg_b, w_out_ab, w_qkv,
           sb_bias, w_o, w_gate, w_up, w_down):
    p = dict(norm_mix_pre=norm_mix_pre, norm_mix_post=norm_mix_post, norm_ffn_pre=norm_ffn_pre,
             norm_ffn_post=norm_ffn_post, w_in_ab=w_in_ab, conv_w=conv_w, sg_ln_g=sg_ln_g,
             sg_ln_b=sg_ln_b, sg_w=sg_w, sg_b=sg_b, w_out_ab=w_out_ab, w_qkv=w_qkv, sb_bias=sb_bias,
             w_o=w_o, w_gate=w_gate, w_up=w_up, w_down=w_down)
    n_odd, n_pool = cache_k.shape[0], cache_k.shape[1]
    n_even = state_conv.shape[0]

    bp = x_prompt.shape[0]
    conv0 = jnp.zeros((n_even, bp, CONV_W - 1, CONV_DIM), x_prompt.dtype)

    def attend_prompt(i, qkv):
        return sb_attn_prompt(qkv, sb_bias[i], tq=256)

    y_prompt, conv_p, sgv_p, k_p, v_p = _trunk(x_prompt, conv0, attend_prompt, p, bm=1024, even_rows=256)

    ck = cache_k.reshape(n_odd, n_pool, PAGE_SIZE, D_MODEL)
    cv = cache_v.reshape(n_odd, n_pool, PAGE_SIZE, D_MODEL)

    def attend_sample(i, qkv):
        d = D_MODEL
        return sb_attn_sample(qkv[:, :, 0:d], qkv[:, :, d:2 * d], qkv[:, :, 2 * d:], ck, cv, page_table,
                              sb_bias[i], i)

    y_sample, conv_s, sgv_s, k_s, v_s = _trunk(x_sample, state_conv, attend_sample, p, bm=1024,
                                               even_rows=256)
    return (y_prompt, y_sample, conv_p, conv_s, sgv_p, sgv_s, k_p, v_p, k_s, v_s)
```

```python
import functools

import jax
import jax.numpy as jnp
from jax import lax
from jax.experimental import pallas as pl
from jax.experimental.pallas import tpu as pltpu

D_MODEL = 2048
DEPTH = 4
PAGE_SIZE = 128
CONV_DIM = D_MODEL // 2
CONV_W = 3
SG_DIM = D_MODEL // 2
SG_HEADS = 8
SG_HEAD_DIM = SG_DIM // SG_HEADS
CHUNK = 128
SB_HEADS = 16
SB_HEAD_DIM = D_MODEL // SB_HEADS
D_FF = 5632
IN_AB = 3 * CONV_DIM + 2 * SG_DIM
EPS = 1e-6

V7X_LANES = 128
V7X_SUBLANES = 8
V7X_VMEM_LIMIT_BYTES = 56 * 1024 * 1024

BF16 = jnp.bfloat16
F32 = jnp.float32

HEADS_PER_STEP = 2
KEYS_PER_TILE = V7X_LANES // SB_HEADS
NT_DIMS = (((1,), (1,)), ((), ()))


def _cparams(n_axes):
    return pltpu.CompilerParams(
        dimension_semantics=("arbitrary",) * n_axes,
        vmem_limit_bytes=V7X_VMEM_LIMIT_BYTES,
    )


def _row_block(m, want):
    return want if m % want == 0 else m


def _rms_cast_kernel(x_ref, g_ref, h_ref):
    x = x_ref[...]
    y = x * lax.rsqrt(jnp.mean(x * x, axis=-1, keepdims=True) + EPS)
    h_ref[...] = (y * g_ref[...]).astype(h_ref.dtype)


def rms_cast(x, g):
    m, d = x.shape
    bm = _row_block(m, 512)
    return pl.pallas_call(
        _rms_cast_kernel,
        out_shape=jax.ShapeDtypeStruct((m, d), BF16),
        grid=(m // bm,),
        in_specs=[pl.BlockSpec((bm, d), lambda i: (i, 0)),
                  pl.BlockSpec((1, d), lambda i: (0, 0))],
        out_specs=pl.BlockSpec((bm, d), lambda i: (i, 0)),
        compiler_params=_cparams(1),
        name="rms_cast",
    )(x, g.reshape(1, d))


def _post_norm_kernel(x_ref, m_ref, gp_ref, gn_ref, xo_ref, *maybe_h_ref):
    m = m_ref[...].astype(F32)
    y = m * lax.rsqrt(jnp.mean(m * m, axis=-1, keepdims=True) + EPS)
    xn = x_ref[...] + y * gp_ref[...]
    xo_ref[...] = xn
    for h_ref in maybe_h_ref:
        hn = xn * lax.rsqrt(jnp.mean(xn * xn, axis=-1, keepdims=True) + EPS)
        h_ref[...] = (hn * gn_ref[...]).astype(h_ref.dtype)


def post_norm(x, m, g_post, g_next):
    rows, d = x.shape
    bm = _row_block(rows, 512)
    with_next = g_next is not None
    row_spec = pl.BlockSpec((bm, d), lambda i: (i, 0))
    gain_spec = pl.BlockSpec((1, d), lambda i: (0, 0))
    out_shape = [jax.ShapeDtypeStruct((rows, d), F32)]
    if with_next:
        out_shape.append(jax.ShapeDtypeStruct((rows, d), BF16))
    else:
        g_next = g_post
    out = pl.pallas_call(
        _post_norm_kernel,
        out_shape=tuple(out_shape),
        grid=(rows // bm,),
        in_specs=[row_spec, row_spec, gain_spec, gain_spec],
        out_specs=tuple([row_spec] * len(out_shape)),
        compiler_params=_cparams(1),
        name="post_norm",
    )(x, m, g_post.reshape(1, d), g_next.reshape(1, d))
    return (out[0], out[1]) if with_next else (out[0], None)


def _mm_kernel(x_ref, w_ref, *rest):
    o_ref, wb_ref = rest[-2:]

    @pl.when(pl.program_id(1) == 0)
    def _():
        wb_ref[...] = w_ref[...].astype(BF16)

    o_ref[...] = jnp.dot(x_ref[...].astype(BF16), wb_ref[...],
                         preferred_element_type=F32).astype(o_ref.dtype)


def matmul(x, w, layer, *, bm, bn, out_dtype=F32, col0=0, n_cols=None, into=None):
    m, k = x.shape
    n_cols = w.shape[2] if n_cols is None else n_cols
    bm = _row_block(m, bm)
    jb0 = col0 // bn
    in_specs = [pl.BlockSpec((bm, k), lambda j, i: (i, 0)),
                pl.BlockSpec((None, k, bn), lambda j, i: (layer, 0, jb0 + j))]
    args = [x, w]
    if into is None:
        out_shape = jax.ShapeDtypeStruct((m, n_cols), out_dtype)
        out_spec = pl.BlockSpec((bm, bn), lambda j, i: (i, j))
        aliases = {}
    else:
        buf, slot = into
        out_shape = jax.ShapeDtypeStruct(buf.shape, buf.dtype)
        out_spec = pl.BlockSpec((None, bm, bn), lambda j, i: (slot, i, j))
        in_specs.append(pl.BlockSpec(memory_space=pl.ANY))
        args.append(buf)
        aliases = {2: 0}
    return pl.pallas_call(
        _mm_kernel,
        out_shape=out_shape,
        grid=(n_cols // bn, m // bm),
        in_specs=in_specs,
        out_specs=out_spec,
        scratch_shapes=[pltpu.VMEM((k, bn), BF16)],
        input_output_aliases=aliases,
        compiler_params=_cparams(2),
        name="matmul",
    )(*args)


def _swiglu_up_kernel(x_ref, wg_ref, wu_ref, o_ref, wgb_ref, wub_ref):
    @pl.when(pl.program_id(1) == 0)
    def _():
        wgb_ref[...] = wg_ref[...].astype(BF16)
        wub_ref[...] = wu_ref[...].astype(BF16)

    x = x_ref[...]
    g = jnp.dot(x, wgb_ref[...], preferred_element_type=F32)
    u = jnp.dot(x, wub_ref[...], preferred_element_type=F32)
    o_ref[...] = (g * jax.nn.sigmoid(g) * u).astype(o_ref.dtype)


def swiglu_up(x, wg, wu, layer, *, bm, bn):
    m, k = x.shape
    n = wg.shape[2]
    bm = _row_block(m, bm)
    return pl.pallas_call(
        _swiglu_up_kernel,
        out_shape=jax.ShapeDtypeStruct((m, n), BF16),
        grid=(n // bn, m // bm),
        in_specs=[pl.BlockSpec((bm, k), lambda j, i: (i, 0)),
                  pl.BlockSpec((None, k, bn), lambda j, i: (layer, 0, j)),
                  pl.BlockSpec((None, k, bn), lambda j, i: (layer, 0, j))],
        out_specs=pl.BlockSpec((bm, bn), lambda j, i: (i, j)),
        scratch_shapes=[pltpu.VMEM((k, bn), BF16), pltpu.VMEM((k, bn), BF16)],
        compiler_params=_cparams(2),
        name="swiglu_up",
    )(x, wg, wu)


def _even_mix_kernel(gb_ref, gc_ref, xa_ref, u_ref, vv_ref, prefix_ref, cw_ref, lng_ref, lnb_ref,
                     sgw_ref, sgbt_ref, y_ref, cnew_ref, vlast_ref, cbuf_ref, wtri_ref, pad_ref,
                     *, rows, n_steps):
    b = pl.program_id(0)
    l = pl.program_id(1)
    carry_at = V7X_SUBLANES - (CONV_W - 1)

    @pl.when((b == 0) & (l == 0))
    def _():
        t = lax.broadcasted_iota(jnp.int32, (CHUNK, CHUNK), 0)
        s = lax.broadcasted_iota(jnp.int32, (CHUNK, CHUNK), 1)
        for h in range(SG_HEADS):
            wtri_ref[h] = jnp.where(s <= t, sgw_ref[h], 0.0).astype(BF16)

    @pl.when(l == 0)
    def _():
        cbuf_ref[carry_at:V7X_SUBLANES, :] = prefix_ref[0]

    c = gc_ref[0] * xa_ref[0]
    cbuf_ref[V7X_SUBLANES:V7X_SUBLANES + rows, :] = c
    conv = cw_ref[CONV_W - 1:CONV_W, :] * c
    for j in range(CONV_W - 1):
        conv = conv + cw_ref[j:j + 1, :] * cbuf_ref[carry_at + j:carry_at + j + rows, :]
    y_ref[0, :, 0:CONV_DIM] = (gb_ref[0] * conv).astype(y_ref.dtype)
    new_hist = cbuf_ref[V7X_SUBLANES + rows - (CONV_W - 1):V7X_SUBLANES + rows, :]
    cbuf_ref[carry_at:V7X_SUBLANES, :] = new_hist

    @pl.when(l == n_steps - 1)
    def _():
        cnew_ref[0] = new_hist

    vv = vv_ref[0]
    mu = jnp.mean(vv, axis=-1, keepdims=True)
    xc = vv - mu
    vn = xc * lax.rsqrt(jnp.mean(xc * xc, axis=-1, keepdims=True) + EPS) * lng_ref[...] + lnb_ref[...]

    n_chunks = -(-rows // CHUNK)
    last_rows = rows - (n_chunks - 1) * CHUNK

    @pl.when(l == n_steps - 1)
    def _():
        vlast_ref[0] = vn[(n_chunks - 1) * CHUNK:, :]

    if rows % CHUNK != 0:
        pad_ref[...] = jnp.zeros_like(pad_ref)
        pad_ref[0:rows, :] = vn
        vn_full = pad_ref[...]
    else:
        vn_full = vn
    vnb = vn_full.astype(BF16)
    u = u_ref[0]
    for ci in range(n_chunks):
        r0 = ci * CHUNK
        nr = CHUNK if ci < n_chunks - 1 else last_rows
        for h in range(SG_HEADS):
            c0 = h * SG_HEAD_DIM
            s = jnp.dot(wtri_ref[h], vnb[r0:r0 + CHUNK, c0:c0 + SG_HEAD_DIM], preferred_element_type=F32)
            s = s + sgbt_ref[:, h:h + 1]
            yb = u[r0:r0 + nr, c0:c0 + SG_HEAD_DIM] * s[0:nr]
            y_ref[0, r0:r0 + nr, CONV_DIM + c0:CONV_DIM + c0 + SG_HEAD_DIM] = yb.astype(y_ref.dtype)


def even_mix(proj, prefix, conv_w, ln_g, ln_b, sg_w, sg_b, *, rows):
    bn, seq, _ = proj.shape
    rows = min(rows, seq)
    n_steps = seq // rows
    r_last = seq - ((seq - 1) // CHUNK) * CHUNK
    pad_rows = -(-rows // CHUNK) * CHUNK

    def col(cb):
        return pl.BlockSpec((1, rows, CONV_DIM), lambda b, l: (b, l, cb))

    const2 = lambda b, l: (0, 0)
    y_dtype = BF16 if rows % 16 == 0 else F32
    return pl.pallas_call(
        functools.partial(_even_mix_kernel, rows=rows, n_steps=n_steps),
        out_shape=(jax.ShapeDtypeStruct((bn, seq, D_MODEL), y_dtype),
                   jax.ShapeDtypeStruct((bn, CONV_W - 1, CONV_DIM), F32),
                   jax.ShapeDtypeStruct((bn, r_last, SG_DIM), F32)),
        grid=(bn, n_steps),
        in_specs=[col(0), col(1), col(2), col(3), col(4),
                  pl.BlockSpec((1, CONV_W - 1, CONV_DIM), lambda b, l: (b, 0, 0)),
                  pl.BlockSpec((CONV_W, CONV_DIM), const2),
                  pl.BlockSpec((1, SG_DIM), const2),
                  pl.BlockSpec((1, SG_DIM), const2),
                  pl.BlockSpec((SG_HEADS, CHUNK, CHUNK), lambda b, l: (0, 0, 0)),
                  pl.BlockSpec((CHUNK, SG_HEADS), const2)],
        out_specs=(pl.BlockSpec((1, rows, D_MODEL), lambda b, l: (b, l, 0)),
                   pl.BlockSpec((1, CONV_W - 1, CONV_DIM), lambda b, l: (b, 0, 0)),
                   pl.BlockSpec((1, r_last, SG_DIM), lambda b, l: (b, 0, 0))),
        scratch_shapes=[pltpu.VMEM((V7X_SUBLANES + rows, CONV_DIM), F32),
                        pltpu.VMEM((SG_HEADS, CHUNK, CHUNK), BF16),
                        pltpu.VMEM((pad_rows, SG_DIM), F32)],
        compiler_params=_cparams(2),
        name="even_mix",
    )(proj, proj, proj, proj, proj, prefix, conv_w, ln_g.reshape(1, SG_DIM), ln_b.reshape(1, SG_DIM),
      sg_w, sg_b.T)


def _softplus_terms(z):
    sp = jnp.maximum(z, 0.0) + jnp.log(1.0 + jnp.exp(-jnp.abs(z)))
    return sp, z - sp


def _split_bf16(x):
    hi = x.astype(BF16)
    lo = (x - hi.astype(F32)).astype(BF16)
    return hi, lo


def _sb_prompt_kernel(bias_ref, q_ref, k_ref, v_ref, o_ref, kb_ref, vb_ref, later_ref, *, tq, tk):
    b = pl.program_id(0)
    hp = pl.program_id(1)
    i = pl.program_id(2)
    blocks_per_tile = tq // tk

    @pl.when((b == 0) & (hp == 0) & (i == 0))
    def _():
        row = lax.broadcasted_iota(jnp.int32, (tk, tk), 0)
        col = lax.broadcasted_iota(jnp.int32, (tk, tk), 1)
        later = jnp.where(row > col, 1.0, 0.0).astype(BF16)
        later_ref[0:tk, :] = later
        later_ref[tk:2 * tk, :] = later

    @pl.when(i == 0)
    def _():
        kb_ref[...] = k_ref[0].astype(BF16)
        vb_ref[...] = v_ref[0].astype(BF16)

    scale = SB_HEAD_DIM ** -0.5
    q = q_ref[0].astype(BF16)

    def block(g, j, acc, run, first_row):
        lanes = slice(g * SB_HEAD_DIM, (g + 1) * SB_HEAD_DIM)
        start = pl.multiple_of(j * tk, tk)
        ks = kb_ref[pl.ds(start, tk), lanes]
        vs = vb_ref[pl.ds(start, tk), lanes]
        z = lax.dot_general(q[:, lanes], ks, NT_DIMS, preferred_element_type=F32) * scale
        z = z + bias_ref[hp * HEADS_PER_STEP + g]
        sp, log_beta = _softplus_terms(z)
        if first_row is not None:
            row = lax.broadcasted_iota(jnp.int32, (tq, tk), 0)
            col = lax.broadcasted_iota(jnp.int32, (tq, tk), 1)
            visible = col + first_row < row
            sp = jnp.where(visible, sp, 0.0)
        hi, lo = _split_bf16(sp)
        behind = jnp.dot(jnp.concatenate([hi, lo], axis=1), later_ref[...], preferred_element_type=F32)
        p = jnp.exp(log_beta - behind - run)
        if first_row is not None:
            p = jnp.where(visible, p, 0.0)
        acc = acc + jnp.dot(p.astype(BF16), vs, preferred_element_type=F32)
        run = run + (behind[:, 0:1] + sp[:, 0:1])
        return acc, run

    carry = []
    for g in range(HEADS_PER_STEP):
        acc, run = jnp.zeros((tq, SB_HEAD_DIM), F32), jnp.zeros((tq, 1), F32)
        for d in reversed(range(blocks_per_tile)):
            acc, run = block(g, i * blocks_per_tile + d, acc, run, d * tk)
        carry.extend((acc, run))

    def body(step, carry):
        out = []
        for g in range(HEADS_PER_STEP):
            out.extend(block(g, i * blocks_per_tile - 1 - step, carry[2 * g], carry[2 * g + 1], None))
        return tuple(out)

    carry = lax.fori_loop(0, i * blocks_per_tile, body, tuple(carry))
    for g in range(HEADS_PER_STEP):
        o_ref[0, :, g * SB_HEAD_DIM:(g + 1) * SB_HEAD_DIM] = carry[2 * g].astype(o_ref.dtype)


def sb_attn_prompt(q, kbuf, vbuf, layer, bias, *, tq, tk):
    bn, seq, _ = q.shape
    nq = seq // tq
    width = HEADS_PER_STEP * SB_HEAD_DIM
    kv_spec = pl.BlockSpec((None, 1, seq, width), lambda b, hp, i: (layer, b, 0, hp))
    return pl.pallas_call(
        functools.partial(_sb_prompt_kernel, tq=tq, tk=tk),
        out_shape=jax.ShapeDtypeStruct((bn, seq, D_MODEL), BF16),
        grid=(bn, SB_HEADS // HEADS_PER_STEP, nq),
        in_specs=[pl.BlockSpec(memory_space=pltpu.SMEM),
                  pl.BlockSpec((1, tq, width), lambda b, hp, i: (b, i, hp)),
                  kv_spec, kv_spec],
        out_specs=pl.BlockSpec((1, tq, width), lambda b, hp, i: (b, i, hp)),
        scratch_shapes=[pltpu.VMEM((seq, width), BF16), pltpu.VMEM((seq, width), BF16),
                        pltpu.VMEM((2 * tk, tk), BF16)],
        compiler_params=_cparams(3),
        name="sb_attn_prompt",
    )(bias, q, kbuf, vbuf)


def _head_of(pair_index):
    return pair_index & (SB_HEADS - 1)


def _key_of(pair_index):
    return pair_index >> (SB_HEADS.bit_length() - 1)


def _sb_sample_kernel(pt_ref, q_ref, bias_ref, knew_ref, vnew_ref, kpg_ref, vpg_ref, o_ref,
                      acc_ref, run_ref, new_ref, scanw_ref, *, n_new):
    b = pl.program_id(0)
    s = pl.program_id(1)
    n_steps = pl.num_programs(1)
    scale = SB_HEAD_DIM ** -0.5
    n_rows = n_new * SB_HEADS

    @pl.when((b == 0) & (s == 0))
    def _():
        r = lax.broadcasted_iota(jnp.int32, (V7X_LANES, 2 * V7X_LANES), 0)
        c = lax.broadcasted_iota(jnp.int32, (V7X_LANES, 2 * V7X_LANES), 1)
        same_head = _head_of(r) == _head_of(c)
        after = _key_of(r) > _key_of(c & (V7X_LANES - 1))
        w = jnp.where(same_head & (after | (c >= V7X_LANES)), 1.0, 0.0).astype(BF16)
        scanw_ref[0:V7X_LANES, :] = w
        scanw_ref[V7X_LANES:2 * V7X_LANES, :] = w

    head_row = lax.broadcasted_iota(jnp.int32, (SB_HEADS, V7X_LANES), 0)
    head_lane = _head_of(lax.broadcasted_iota(jnp.int32, (SB_HEADS, V7X_LANES), 1))
    own = head_row == head_lane
    query_row = lax.broadcasted_iota(jnp.int32, (V7X_SUBLANES, V7X_LANES), 0)

    def absorb(k2d, v2d, n_tiles, visible):
        zfull = lax.dot_general(q_ref[0], k2d, NT_DIMS, preferred_element_type=F32)
        tiles = []
        for t in range(n_tiles):
            lanes = slice(t * V7X_LANES, (t + 1) * V7X_LANES)
            tile = jnp.zeros((V7X_SUBLANES, V7X_LANES), F32)
            for i in range(n_new):
                own_head = jnp.sum(jnp.where(own, zfull[i * SB_HEADS:(i + 1) * SB_HEADS, lanes], 0.0),
                                   axis=0, keepdims=True)
                tile = jnp.where(query_row == i, own_head, tile)
            tiles.append(tile)
        z = jnp.concatenate(tiles, axis=0) * scale + bias_ref[...]
        sp, log_beta = _softplus_terms(z)
        if visible is not None:
            sp = jnp.where(visible, sp, 0.0)
        hi, lo = _split_bf16(sp)
        scan = jnp.dot(jnp.concatenate([hi, lo], axis=1), scanw_ref[...], preferred_element_type=F32)
        run = run_ref[...]
        p_tiles = [None] * n_tiles
        for t in reversed(range(n_tiles)):
            rs = slice(t * V7X_SUBLANES, (t + 1) * V7X_SUBLANES)
            p_t = jnp.exp(log_beta[rs] - scan[rs, 0:V7X_LANES] - run)
            if visible is not None:
                p_t = jnp.where(visible[rs], p_t, 0.0)
            p_tiles[t] = p_t
            run = run + scan[rs, V7X_LANES:2 * V7X_LANES]
        run_ref[...] = run
        blocks = []
        for i in range(n_new):
            blocks.append(jnp.concatenate(
                [jnp.where(own, jnp.broadcast_to(p_tiles[t][i:i + 1, :], (SB_HEADS, V7X_LANES)), 0.0)
                 for t in range(n_tiles)], axis=1).astype(BF16))
        pbd = jnp.concatenate(blocks, axis=0)
        acc_ref[...] += jnp.dot(pbd, v2d, preferred_element_type=F32)

    @pl.when(s == 0)
    def _():
        acc_ref[...] = jnp.zeros_like(acc_ref)
        run_ref[...] = jnp.zeros_like(run_ref)
        new_ref[...] = jnp.zeros_like(new_ref)
        new_ref[0, 0:n_new] = knew_ref[0]
        new_ref[1, 0:n_new] = vnew_ref[0]
        key = _key_of(lax.broadcasted_iota(jnp.int32, (V7X_SUBLANES, V7X_LANES), 1))
        absorb(new_ref[0].reshape(V7X_LANES, SB_HEAD_DIM).astype(BF16),
               new_ref[1].reshape(V7X_LANES, SB_HEAD_DIM).astype(BF16), 1, key < query_row)

    @pl.when(s > 0)
    def _():
        n_rows_page = PAGE_SIZE * SB_HEADS
        absorb(kpg_ref[...].reshape(n_rows_page, SB_HEAD_DIM).astype(BF16),
               vpg_ref[...].reshape(n_rows_page, SB_HEAD_DIM).astype(BF16),
               n_rows_page // V7X_LANES, None)

    @pl.when(s == n_steps - 1)
    def _():
        for i in range(n_new):
            for h in range(SB_HEADS):
                r = i * SB_HEADS + h
                o_ref[0, i:i + 1, h * SB_HEAD_DIM:(h + 1) * SB_HEAD_DIM] = acc_ref[r:r + 1, :]


def sb_attn_sample(q, kbuf, vbuf, layer, cache_k, cache_v, page_table, bias):
    bn, n_new, _ = q.shape
    assert n_new <= V7X_SUBLANES and KEYS_PER_TILE >= n_new
    n_layers = kbuf.shape[0]
    n_pages = page_table.shape[1]
    n_rows = n_new * SB_HEADS
    q_rows = q.reshape(bn, n_rows, SB_HEAD_DIM).astype(BF16)
    bias_l = jnp.tile(bias, KEYS_PER_TILE).reshape(1, V7X_LANES)
    k_new = kbuf.reshape(n_layers, bn, n_new, SB_HEADS, SB_HEAD_DIM)
    v_new = vbuf.reshape(n_layers, bn, n_new, SB_HEADS, SB_HEAD_DIM)

    def page(b, s, pt):
        return (layer, pt[b, n_pages - jnp.maximum(s, 1)], 0, 0, 0)

    new_spec = pl.BlockSpec((None, 1, n_new, SB_HEADS, SB_HEAD_DIM), lambda b, s, pt: (layer, b, 0, 0, 0))
    page_spec = pl.BlockSpec((None, None, PAGE_SIZE, SB_HEADS, SB_HEAD_DIM), page)
    grid_spec = pltpu.PrefetchScalarGridSpec(
        num_scalar_prefetch=1,
        grid=(bn, n_pages + 1),
        in_specs=[pl.BlockSpec((1, n_rows, SB_HEAD_DIM), lambda b, s, pt: (b, 0, 0)),
                  pl.BlockSpec((1, V7X_LANES), lambda b, s, pt: (0, 0)),
                  new_spec, new_spec, page_spec, page_spec],
        out_specs=pl.BlockSpec((1, n_new, D_MODEL), lambda b, s, pt: (b, 0, 0)),
        scratch_shapes=[pltpu.VMEM((n_rows, SB_HEAD_DIM), F32),
                        pltpu.VMEM((V7X_SUBLANES, V7X_LANES), F32),
                        pltpu.VMEM((2, KEYS_PER_TILE, SB_HEADS, SB_HEAD_DIM), F32),
                        pltpu.VMEM((2 * V7X_LANES, 2 * V7X_LANES), BF16)],
    )
    return pl.pallas_call(
        functools.partial(_sb_sample_kernel, n_new=n_new),
        out_shape=jax.ShapeDtypeStruct((bn, n_new, D_MODEL), F32),
        grid_spec=grid_spec,
        compiler_params=_cparams(2),
        name="sb_attn_sample",
    )(page_table, q_rows, bias_l, k_new, v_new, cache_k, cache_v)


def _trunk(x3, conv_state, attend, p, *, bm, even_rows):
    bn, seq, d = x3.shape
    rows = bn * seq
    n_odd = p['w_qkv'].shape[0]
    x = x3.reshape(rows, d)
    h = rms_cast(x, p['norm_mix_pre'][0])
    conv_new, sgv_new = [], []
    kbuf = jnp.zeros((n_odd, rows, d), F32)
    vbuf = jnp.zeros((n_odd, rows, d), F32)
    for l in range(DEPTH):
        i = l // 2
        if l % 2 == 0:
            proj = matmul(h, p['w_in_ab'], i, bm=bm, bn=1024)
            y, c, vr = even_mix(proj.reshape(bn, seq, IN_AB), conv_state[i], p['conv_w'][i], p['sg_ln_g'][i],
                                p['sg_ln_b'][i], p['sg_w'][i], p['sg_b'][i], rows=even_rows)
            conv_new.append(c)
            sgv_new.append(vr)
            m = matmul(y.reshape(rows, d), p['w_out_ab'], i, bm=bm, bn=1024)
        else:
            q = matmul(h, p['w_qkv'], i, bm=bm, bn=1024, col0=0, n_cols=d)
            kbuf = matmul(h, p['w_qkv'], i, bm=bm, bn=1024, col0=d, n_cols=d, into=(kbuf, i))
            vbuf = matmul(h, p['w_qkv'], i, bm=bm, bn=1024, col0=2 * d, n_cols=d, into=(vbuf, i))
            o = attend(i, q.reshape(bn, seq, d), kbuf.reshape(n_odd, bn, seq, d), vbuf.reshape(n_odd, bn, seq, d))
            m = matmul(o.reshape(rows, d), p['w_o'], i, bm=bm, bn=1024)
        x, h = post_norm(x, m, p['norm_mix_post'][l], p['norm_ffn_pre'][l])
        a = swiglu_up(h, p['w_gate'], p['w_up'], l, bm=bm, bn=512)
        m = matmul(a, p['w_down'], l, bm=min(bm, 512), bn=512)
        x, h = post_norm(x, m, p['norm_ffn_post'][l], p['norm_mix_pre'][l + 1] if l + 1 < DEPTH else None)
    kv_shape = (n_odd, bn, seq, SB_HEADS, SB_HEAD_DIM)
    return (x.reshape(bn, seq, d), jnp.stack(conv_new), jnp.stack(sgv_new),
            kbuf.reshape(kv_shape), vbuf.reshape(kv_shape))


def kernel(x_prompt, x_sample, state_conv, cache_k, cache_v, page_table, norm_mix_pre, norm_mix_post,
           norm_ffn_pre, norm_ffn_post, w_in_ab, conv_w, sg_ln_g, sg_ln_b, sg_w, sg_b, w_out_ab, w_qkv,
           sb_bias, w_o, w_gate, w_up, w_down):
    p = dict(norm_mix_pre=norm_mix_pre, norm_mix_post=norm_mix_post, norm_ffn_pre=norm_ffn_pre,
             norm_ffn_post=norm_ffn_post, w_in_ab=w_in_ab, conv_w=conv_w, sg_ln_g=sg_ln_g,
             sg_ln_b=sg_ln_b, sg_w=sg_w, sg_b=sg_b, w_out_ab=w_out_ab, w_qkv=w_qkv, sb_bias=sb_bias,
             w_o=w_o, w_gate=w_gate, w_up=w_up, w_down=w_down)
    n_even = state_conv.shape[0]

    bp = x_prompt.shape[0]
    conv0 = jnp.zeros((n_even, bp, CONV_W - 1, CONV_DIM), x_prompt.dtype)

    def attend_prompt(i, q, kbuf, vbuf):
        return sb_attn_prompt(q, kbuf, vbuf, i, sb_bias[i], tq=512, tk=256)

    y_prompt, conv_p, sgv_p, k_p, v_p = _trunk(x_prompt, conv0, attend_prompt, p, bm=1024, even_rows=256)

    def attend_sample(i, q, kbuf, vbuf):
        return sb_attn_sample(q, kbuf, vbuf, i, cache_k, cache_v, page_table, sb_bias[i])

    y_sample, conv_s, sgv_s, k_s, v_s = _trunk(x_sample, state_conv, attend_sample, p, bm=1024,
                                               even_rows=256)
    return (y_prompt, y_sample, conv_p, conv_s, sgv_p, sgv_s, k_p, v_p, k_s, v_s)
```

```python
import functools

import jax
import jax.numpy as jnp
from jax import lax
from jax.experimental import pallas as pl
from jax.experimental.pallas import tpu as pltpu

D_MODEL = 2048
DEPTH = 4
PAGE_SIZE = 128
CONV_DIM = D_MODEL // 2
CONV_W = 3
SG_DIM = D_MODEL // 2
SG_HEADS = 8
SG_HEAD_DIM = SG_DIM // SG_HEADS
CHUNK = 128
SB_HEADS = 16
SB_HEAD_DIM = D_MODEL // SB_HEADS
D_FF = 5632
IN_AB = 3 * CONV_DIM + 2 * SG_DIM
EPS = 1e-6

V7X_LANES = 128
V7X_SUBLANES = 8
V7X_VMEM_LIMIT_BYTES = 56 * 1024 * 1024

BF16 = jnp.bfloat16
F32 = jnp.float32

HEADS_PER_STEP = 2
SAMPLE_PAGES_PER_STEP = 8
KEYS_PER_TILE = V7X_LANES // SB_HEADS
NT_DIMS = (((1,), (1,)), ((), ()))


def _cparams(n_axes):
    return pltpu.CompilerParams(
        dimension_semantics=("arbitrary",) * n_axes,
        vmem_limit_bytes=V7X_VMEM_LIMIT_BYTES,
    )


def _row_block(m, want):
    return want if m % want == 0 else m


def _rms_cast_kernel(x_ref, g_ref, h_ref):
    x = x_ref[...]
    y = x * lax.rsqrt(jnp.mean(x * x, axis=-1, keepdims=True) + EPS)
    h_ref[...] = (y * g_ref[...]).astype(h_ref.dtype)


def rms_cast(x, g):
    m, d = x.shape
    bm = _row_block(m, 512)
    return pl.pallas_call(
        _rms_cast_kernel,
        out_shape=jax.ShapeDtypeStruct((m, d), BF16),
        grid=(m // bm,),
        in_specs=[pl.BlockSpec((bm, d), lambda i: (i, 0)),
                  pl.BlockSpec((1, d), lambda i: (0, 0))],
        out_specs=pl.BlockSpec((bm, d), lambda i: (i, 0)),
        compiler_params=_cparams(1),
        name="rms_cast",
    )(x, g.reshape(1, d))


def _post_norm_kernel(x_ref, m_ref, gp_ref, gn_ref, xo_ref, *maybe_h_ref):
    m = m_ref[...].astype(F32)
    y = m * lax.rsqrt(jnp.mean(m * m, axis=-1, keepdims=True) + EPS)
    xn = x_ref[...] + y * gp_ref[...]
    xo_ref[...] = xn
    for h_ref in maybe_h_ref:
        hn = xn * lax.rsqrt(jnp.mean(xn * xn, axis=-1, keepdims=True) + EPS)
        h_ref[...] = (hn * gn_ref[...]).astype(h_ref.dtype)


def post_norm(x, m, g_post, g_next):
    rows, d = x.shape
    bm = _row_block(rows, 512)
    with_next = g_next is not None
    row_spec = pl.BlockSpec((bm, d), lambda i: (i, 0))
    gain_spec = pl.BlockSpec((1, d), lambda i: (0, 0))
    out_shape = [jax.ShapeDtypeStruct((rows, d), F32)]
    if with_next:
        out_shape.append(jax.ShapeDtypeStruct((rows, d), BF16))
    else:
        g_next = g_post
    out = pl.pallas_call(
        _post_norm_kernel,
        out_shape=tuple(out_shape),
        grid=(rows // bm,),
        in_specs=[row_spec, row_spec, gain_spec, gain_spec],
        out_specs=tuple([row_spec] * len(out_shape)),
        compiler_params=_cparams(1),
        name="post_norm",
    )(x, m, g_post.reshape(1, d), g_next.reshape(1, d))
    return (out[0], out[1]) if with_next else (out[0], None)


def _mm_kernel(x_ref, w_ref, *rest):
    o_ref, wb_ref = rest[-2:]

    @pl.when(pl.program_id(1) == 0)
    def _():
        wb_ref[...] = w_ref[...].astype(BF16)

    o_ref[...] = jnp.dot(x_ref[...].astype(BF16), wb_ref[...],
                         preferred_element_type=F32).astype(o_ref.dtype)


def matmul(x, w, layer, *, bm, bn, out_dtype=F32, col0=0, n_cols=None, into=None):
    m, k = x.shape
    n_cols = w.shape[2] if n_cols is None else n_cols
    bm = _row_block(m, bm)
    jb0 = col0 // bn
    in_specs = [pl.BlockSpec((bm, k), lambda j, i: (i, 0)),
                pl.BlockSpec((None, k, bn), lambda j, i: (layer, 0, jb0 + j))]
    args = [x, w]
    if into is None:
        out_shape = jax.ShapeDtypeStruct((m, n_cols), out_dtype)
        out_spec = pl.BlockSpec((bm, bn), lambda j, i: (i, j))
        aliases = {}
    else:
        buf, slot = into
        out_shape = jax.ShapeDtypeStruct(buf.shape, buf.dtype)
        out_spec = pl.BlockSpec((None, bm, bn), lambda j, i: (slot, i, j))
        in_specs.append(pl.BlockSpec(memory_space=pl.ANY))
        args.append(buf)
        aliases = {2: 0}
    return pl.pallas_call(
        _mm_kernel,
        out_shape=out_shape,
        grid=(n_cols // bn, m // bm),
        in_specs=in_specs,
        out_specs=out_spec,
        scratch_shapes=[pltpu.VMEM((k, bn), BF16)],
        input_output_aliases=aliases,
        compiler_params=_cparams(2),
        name="matmul",
    )(*args)


def _mm_post_norm_kernel(y_ref, w_ref, x_ref, gp_ref, gn_ref, xo_ref, h_ref, wb_ref):
    @pl.when(pl.program_id(0) == 0)
    def _():
        wb_ref[...] = w_ref[...].astype(BF16)

    m = jnp.dot(y_ref[...].astype(BF16), wb_ref[...], preferred_element_type=F32)
    nm = m * lax.rsqrt(jnp.mean(m * m, axis=-1, keepdims=True) + EPS)
    xn = x_ref[...] + nm * gp_ref[...]
    xo_ref[...] = xn
    hn = xn * lax.rsqrt(jnp.mean(xn * xn, axis=-1, keepdims=True) + EPS)
    h_ref[...] = (hn * gn_ref[...]).astype(h_ref.dtype)


def matmul_post_norm(y, w, layer, x, g_post, g_next, *, bm):
    rows, k = y.shape
    n = w.shape[2]
    bm = _row_block(rows, bm)
    row_spec = pl.BlockSpec((bm, n), lambda i: (i, 0))
    gain_spec = pl.BlockSpec((1, n), lambda i: (0, 0))
    return pl.pallas_call(
        _mm_post_norm_kernel,
        out_shape=(jax.ShapeDtypeStruct((rows, n), F32), jax.ShapeDtypeStruct((rows, n), BF16)),
        grid=(rows // bm,),
        in_specs=[pl.BlockSpec((bm, k), lambda i: (i, 0)),
                  pl.BlockSpec((None, k, n), lambda i: (layer, 0, 0), pipeline_mode=pl.Buffered(1)),
                  row_spec, gain_spec, gain_spec],
        out_specs=(row_spec, row_spec),
        scratch_shapes=[pltpu.VMEM((k, n), BF16)],
        compiler_params=_cparams(1),
        name="matmul_post_norm",
    )(y, w, x, g_post.reshape(1, n), g_next.reshape(1, n))


def _swiglu_up_kernel(x_ref, wg_ref, wu_ref, o_ref, wgb_ref, wub_ref):
    @pl.when(pl.program_id(1) == 0)
    def _():
        wgb_ref[...] = wg_ref[...].astype(BF16)
        wub_ref[...] = wu_ref[...].astype(BF16)

    x = x_ref[...]
    g = jnp.dot(x, wgb_ref[...], preferred_element_type=F32)
    u = jnp.dot(x, wub_ref[...], preferred_element_type=F32)
    o_ref[...] = (g * jax.nn.sigmoid(g) * u).astype(o_ref.dtype)


def swiglu_up(x, wg, wu, layer, *, bm, bn):
    m, k = x.shape
    n = wg.shape[2]
    bm = _row_block(m, bm)
    return pl.pallas_call(
        _swiglu_up_kernel,
        out_shape=jax.ShapeDtypeStruct((m, n), BF16),
        grid=(n // bn, m // bm),
        in_specs=[pl.BlockSpec((bm, k), lambda j, i: (i, 0)),
                  pl.BlockSpec((None, k, bn), lambda j, i: (layer, 0, j)),
                  pl.BlockSpec((None, k, bn), lambda j, i: (layer, 0, j))],
        out_specs=pl.BlockSpec((bm, bn), lambda j, i: (i, j)),
        scratch_shapes=[pltpu.VMEM((k, bn), BF16), pltpu.VMEM((k, bn), BF16)],
        compiler_params=_cparams(2),
        name="swiglu_up",
    )(x, wg, wu)


def _even_mix_kernel(gb_ref, gc_ref, xa_ref, u_ref, vv_ref, prefix_ref, cw_ref, lng_ref, lnb_ref,
                     sgw_ref, sgbt_ref, y_ref, cnew_ref, vlast_ref, cbuf_ref, wtri_ref, pad_ref,
                     *, rows, n_steps):
    b = pl.program_id(0)
    l = pl.program_id(1)
    carry_at = V7X_SUBLANES - (CONV_W - 1)

    @pl.when((b == 0) & (l == 0))
    def _():
        t = lax.broadcasted_iota(jnp.int32, (CHUNK, CHUNK), 0)
        s = lax.broadcasted_iota(jnp.int32, (CHUNK, CHUNK), 1)
        for h in range(SG_HEADS):
            wtri_ref[h] = jnp.where(s <= t, sgw_ref[h], 0.0).astype(BF16)

    @pl.when(l == 0)
    def _():
        cbuf_ref[carry_at:V7X_SUBLANES, :] = prefix_ref[0]

    c = gc_ref[0] * xa_ref[0]
    cbuf_ref[V7X_SUBLANES:V7X_SUBLANES + rows, :] = c
    conv = cw_ref[CONV_W - 1:CONV_W, :] * c
    for j in range(CONV_W - 1):
        conv = conv + cw_ref[j:j + 1, :] * cbuf_ref[carry_at + j:carry_at + j + rows, :]
    y_ref[0, :, 0:CONV_DIM] = (gb_ref[0] * conv).astype(y_ref.dtype)
    new_hist = cbuf_ref[V7X_SUBLANES + rows - (CONV_W - 1):V7X_SUBLANES + rows, :]
    cbuf_ref[carry_at:V7X_SUBLANES, :] = new_hist

    @pl.when(l == n_steps - 1)
    def _():
        cnew_ref[0] = new_hist

    vv = vv_ref[0]
    mu = jnp.mean(vv, axis=-1, keepdims=True)
    xc = vv - mu
    vn = xc * lax.rsqrt(jnp.mean(xc * xc, axis=-1, keepdims=True) + EPS) * lng_ref[...] + lnb_ref[...]

    n_chunks = -(-rows // CHUNK)
    last_rows = rows - (n_chunks - 1) * CHUNK

    @pl.when(l == n_steps - 1)
    def _():
        vlast_ref[0] = vn[(n_chunks - 1) * CHUNK:, :]

    if rows % CHUNK != 0:
        pad_ref[...] = jnp.zeros_like(pad_ref)
        pad_ref[0:rows, :] = vn
        vn_full = pad_ref[...]
    else:
        vn_full = vn
    vnb = vn_full.astype(BF16)
    u = u_ref[0]
    for ci in range(n_chunks):
        r0 = ci * CHUNK
        nr = CHUNK if ci < n_chunks - 1 else last_rows
        for h in range(SG_HEADS):
            c0 = h * SG_HEAD_DIM
            s = jnp.dot(wtri_ref[h], vnb[r0:r0 + CHUNK, c0:c0 + SG_HEAD_DIM], preferred_element_type=F32)
            s = s + sgbt_ref[:, h:h + 1]
            yb = u[r0:r0 + nr, c0:c0 + SG_HEAD_DIM] * s[0:nr]
            y_ref[0, r0:r0 + nr, CONV_DIM + c0:CONV_DIM + c0 + SG_HEAD_DIM] = yb.astype(y_ref.dtype)


def even_mix(proj, prefix, conv_w, ln_g, ln_b, sg_w, sg_b, *, rows):
    bn, seq, _ = proj.shape
    rows = min(rows, seq)
    n_steps = seq // rows
    r_last = seq - ((seq - 1) // CHUNK) * CHUNK
    pad_rows = -(-rows // CHUNK) * CHUNK

    def col(cb):
        return pl.BlockSpec((1, rows, CONV_DIM), lambda b, l: (b, l, cb))

    const2 = lambda b, l: (0, 0)
    y_dtype = BF16 if rows % 16 == 0 else F32
    return pl.pallas_call(
        functools.partial(_even_mix_kernel, rows=rows, n_steps=n_steps),
        out_shape=(jax.ShapeDtypeStruct((bn, seq, D_MODEL), y_dtype),
                   jax.ShapeDtypeStruct((bn, CONV_W - 1, CONV_DIM), F32),
                   jax.ShapeDtypeStruct((bn, r_last, SG_DIM), F32)),
        grid=(bn, n_steps),
        in_specs=[col(0), col(1), col(2), col(3), col(4),
                  pl.BlockSpec((1, CONV_W - 1, CONV_DIM), lambda b, l: (b, 0, 0)),
                  pl.BlockSpec((CONV_W, CONV_DIM), const2),
                  pl.BlockSpec((1, SG_DIM), const2),
                  pl.BlockSpec((1, SG_DIM), const2),
                  pl.BlockSpec((SG_HEADS, CHUNK, CHUNK), lambda b, l: (0, 0, 0)),
                  pl.BlockSpec((CHUNK, SG_HEADS), const2)],
        out_specs=(pl.BlockSpec((1, rows, D_MODEL), lambda b, l: (b, l, 0)),
                   pl.BlockSpec((1, CONV_W - 1, CONV_DIM), lambda b, l: (b, 0, 0)),
                   pl.BlockSpec((1, r_last, SG_DIM), lambda b, l: (b, 0, 0))),
        scratch_shapes=[pltpu.VMEM((V7X_SUBLANES + rows, CONV_DIM), F32),
                        pltpu.VMEM((SG_HEADS, CHUNK, CHUNK), BF16),
                        pltpu.VMEM((pad_rows, SG_DIM), F32)],
        compiler_params=_cparams(2),
        name="even_mix",
    )(proj, proj, proj, proj, proj, prefix, conv_w, ln_g.reshape(1, SG_DIM), ln_b.reshape(1, SG_DIM),
      sg_w, sg_b.T)


def _softplus_terms(z):
    sp = jnp.maximum(z, 0.0) + jnp.log(1.0 + jnp.exp(-jnp.abs(z)))
    return sp, z - sp


def _split_bf16(x):
    hi = x.astype(BF16)
    lo = (x - hi.astype(F32)).astype(BF16)
    return hi, lo


def _sb_prompt_kernel(bias_ref, q_ref, k_ref, v_ref, o_ref, kb_ref, vb_ref, later_ref, *, tq, tk):
    b = pl.program_id(0)
    hp = pl.program_id(1)
    i = pl.program_id(2)
    blocks_per_tile = tq // tk

    @pl.when((b == 0) & (hp == 0) & (i == 0))
    def _():
        row = lax.broadcasted_iota(jnp.int32, (tk, tk), 0)
        col = lax.broadcasted_iota(jnp.int32, (tk, tk), 1)
        later = jnp.where(row > col, 1.0, 0.0).astype(BF16)
        later_ref[0:tk, :] = later
        later_ref[tk:2 * tk, :] = later

    @pl.when(i == 0)
    def _():
        kb_ref[...] = k_ref[0].astype(BF16)
        vb_ref[...] = v_ref[0].astype(BF16)

    scale = SB_HEAD_DIM ** -0.5
    q = q_ref[0].astype(BF16)

    def block(g, j, acc, run, first_row):
        lanes = slice(g * SB_HEAD_DIM, (g + 1) * SB_HEAD_DIM)
        start = pl.multiple_of(j * tk, tk)
        ks = kb_ref[pl.ds(start, tk), lanes]
        vs = vb_ref[pl.ds(start, tk), lanes]
        z = lax.dot_general(q[:, lanes], ks, NT_DIMS, preferred_element_type=F32) * scale
        z = z + bias_ref[hp * HEADS_PER_STEP + g]
        sp, log_beta = _softplus_terms(z)
        if first_row is not None:
            row = lax.broadcasted_iota(jnp.int32, (tq, tk), 0)
            col = lax.broadcasted_iota(jnp.int32, (tq, tk), 1)
            visible = col + first_row < row
            sp = jnp.where(visible, sp, 0.0)
        hi, lo = _split_bf16(sp)
        behind = jnp.dot(jnp.concatenate([hi, lo], axis=1), later_ref[...], preferred_element_type=F32)
        p = jnp.exp(log_beta - behind - run)
        if first_row is not None:
            p = jnp.where(visible, p, 0.0)
        acc = acc + jnp.dot(p.astype(BF16), vs, preferred_element_type=F32)
        run = run + (behind[:, 0:1] + sp[:, 0:1])
        return acc, run

    carry = []
    for g in range(HEADS_PER_STEP):
        acc, run = jnp.zeros((tq, SB_HEAD_DIM), F32), jnp.zeros((tq, 1), F32)
        for d in reversed(range(blocks_per_tile)):
            acc, run = block(g, i * blocks_per_tile + d, acc, run, d * tk)
        carry.extend((acc, run))

    def body(step, carry):
        out = []
        for g in range(HEADS_PER_STEP):
            out.extend(block(g, i * blocks_per_tile - 1 - step, carry[2 * g], carry[2 * g + 1], None))
        return tuple(out)

    carry = lax.fori_loop(0, i * blocks_per_tile, body, tuple(carry))
    for g in range(HEADS_PER_STEP):
        o_ref[0, :, g * SB_HEAD_DIM:(g + 1) * SB_HEAD_DIM] = carry[2 * g].astype(o_ref.dtype)


def sb_attn_prompt(q, kbuf, vbuf, layer, bias, *, tq, tk):
    bn, seq, _ = q.shape
    nq = seq // tq
    width = HEADS_PER_STEP * SB_HEAD_DIM
    kv_spec = pl.BlockSpec((None, 1, seq, width), lambda b, hp, i: (layer, b, 0, hp))
    return pl.pallas_call(
        functools.partial(_sb_prompt_kernel, tq=tq, tk=tk),
        out_shape=jax.ShapeDtypeStruct((bn, seq, D_MODEL), BF16),
        grid=(bn, SB_HEADS // HEADS_PER_STEP, nq),
        in_specs=[pl.BlockSpec(memory_space=pltpu.SMEM),
                  pl.BlockSpec((1, tq, width), lambda b, hp, i: (b, i, hp)),
                  kv_spec, kv_spec],
        out_specs=pl.BlockSpec((1, tq, width), lambda b, hp, i: (b, i, hp)),
        scratch_shapes=[pltpu.VMEM((seq, width), BF16), pltpu.VMEM((seq, width), BF16),
                        pltpu.VMEM((2 * tk, tk), BF16)],
        compiler_params=_cparams(3),
        name="sb_attn_prompt",
    )(bias, q, kbuf, vbuf)


def _head_of(pair_index):
    return pair_index & (SB_HEADS - 1)


def _key_of(pair_index):
    return pair_index >> (SB_HEADS.bit_length() - 1)


def _sb_sample_kernel(pt_ref, q_ref, bias_ref, knew_ref, vnew_ref, *rest, n_new, pages_per_step):
    kpg_refs = rest[0:pages_per_step]
    vpg_refs = rest[pages_per_step:2 * pages_per_step]
    o_ref, acc_ref, run_ref, new_ref, scanw_ref = rest[2 * pages_per_step:]
    b = pl.program_id(0)
    s = pl.program_id(1)
    n_steps = pl.num_programs(1)
    scale = SB_HEAD_DIM ** -0.5
    n_rows = n_new * SB_HEADS

    @pl.when((b == 0) & (s == 0))
    def _():
        r = lax.broadcasted_iota(jnp.int32, (V7X_LANES, 2 * V7X_LANES), 0)
        c = lax.broadcasted_iota(jnp.int32, (V7X_LANES, 2 * V7X_LANES), 1)
        same_head = _head_of(r) == _head_of(c)
        after = _key_of(r) > _key_of(c & (V7X_LANES - 1))
        w = jnp.where(same_head & (after | (c >= V7X_LANES)), 1.0, 0.0).astype(BF16)
        scanw_ref[0:V7X_LANES, :] = w
        scanw_ref[V7X_LANES:2 * V7X_LANES, :] = w

    head_row = lax.broadcasted_iota(jnp.int32, (SB_HEADS, V7X_LANES), 0)
    head_lane = _head_of(lax.broadcasted_iota(jnp.int32, (SB_HEADS, V7X_LANES), 1))
    own = head_row == head_lane
    query_row = lax.broadcasted_iota(jnp.int32, (V7X_SUBLANES, V7X_LANES), 0)

    def absorb(k2d, v2d, n_tiles, visible, acc, run):
        zfull = lax.dot_general(q_ref[0], k2d, NT_DIMS, preferred_element_type=F32)
        tiles = []
        for t in range(n_tiles):
            lanes = slice(t * V7X_LANES, (t + 1) * V7X_LANES)
            tile = jnp.zeros((V7X_SUBLANES, V7X_LANES), F32)
            for i in range(n_new):
                own_head = jnp.sum(jnp.where(own, zfull[i * SB_HEADS:(i + 1) * SB_HEADS, lanes], 0.0),
                                   axis=0, keepdims=True)
                tile = jnp.where(query_row == i, own_head, tile)
            tiles.append(tile)
        z = jnp.concatenate(tiles, axis=0) * scale + bias_ref[...]
        sp, log_beta = _softplus_terms(z)
        if visible is not None:
            sp = jnp.where(visible, sp, 0.0)
        hi, lo = _split_bf16(sp)
        scan = jnp.dot(jnp.concatenate([hi, lo], axis=1), scanw_ref[...], preferred_element_type=F32)
        p_tiles = [None] * n_tiles
        for t in reversed(range(n_tiles)):
            rs = slice(t * V7X_SUBLANES, (t + 1) * V7X_SUBLANES)
            p_t = jnp.exp(log_beta[rs] - scan[rs, 0:V7X_LANES] - run)
            if visible is not None:
                p_t = jnp.where(visible[rs], p_t, 0.0)
            p_tiles[t] = p_t
            run = run + scan[rs, V7X_LANES:2 * V7X_LANES]
        blocks = []
        for i in range(n_new):
            blocks.append(jnp.concatenate(
                [jnp.where(own, jnp.broadcast_to(p_tiles[t][i:i + 1, :], (SB_HEADS, V7X_LANES)), 0.0)
                 for t in range(n_tiles)], axis=1).astype(BF16))
        pbd = jnp.concatenate(blocks, axis=0)
        return acc + jnp.dot(pbd, v2d, preferred_element_type=F32), run

    @pl.when(s == 0)
    def _():
        new_ref[...] = jnp.zeros_like(new_ref)
        new_ref[0, 0:n_new] = knew_ref[0]
        new_ref[1, 0:n_new] = vnew_ref[0]
        key = _key_of(lax.broadcasted_iota(jnp.int32, (V7X_SUBLANES, V7X_LANES), 1))
        acc, run = absorb(new_ref[0].reshape(V7X_LANES, SB_HEAD_DIM).astype(BF16),
                          new_ref[1].reshape(V7X_LANES, SB_HEAD_DIM).astype(BF16), 1, key < query_row,
                          jnp.zeros(acc_ref.shape, F32), jnp.zeros(run_ref.shape, F32))
        acc_ref[...] = acc
        run_ref[...] = run

    @pl.when(s > 0)
    def _():
        n_rows_page = PAGE_SIZE * SB_HEADS
        acc, run = acc_ref[...], run_ref[...]
        for u in reversed(range(pages_per_step)):
            acc, run = absorb(kpg_refs[u][...].reshape(n_rows_page, SB_HEAD_DIM).astype(BF16),
                              vpg_refs[u][...].reshape(n_rows_page, SB_HEAD_DIM).astype(BF16),
                              n_rows_page // V7X_LANES, None, acc, run)
        acc_ref[...] = acc
        run_ref[...] = run

    @pl.when(s == n_steps - 1)
    def _():
        for i in range(n_new):
            for h in range(SB_HEADS):
                r = i * SB_HEADS + h
                o_ref[0, i:i + 1, h * SB_HEAD_DIM:(h + 1) * SB_HEAD_DIM] = acc_ref[r:r + 1, :]


def sb_attn_sample(q, kbuf, vbuf, layer, cache_k, cache_v, page_table, bias):
    bn, n_new, _ = q.shape
    assert n_new <= V7X_SUBLANES and KEYS_PER_TILE >= n_new
    n_layers = kbuf.shape[0]
    n_pages = page_table.shape[1]
    n_rows = n_new * SB_HEADS
    q_rows = q.reshape(bn, n_rows, SB_HEAD_DIM).astype(BF16)
    bias_l = jnp.tile(bias, KEYS_PER_TILE).reshape(1, V7X_LANES)
    k_new = kbuf.reshape(n_layers, bn, n_new, SB_HEADS, SB_HEAD_DIM)
    v_new = vbuf.reshape(n_layers, bn, n_new, SB_HEADS, SB_HEAD_DIM)

    pps = next(c for c in (SAMPLE_PAGES_PER_STEP, 2, 1) if n_pages % c == 0)

    def page_spec(u):
        def index(b, s, pt):
            return (layer, pt[b, n_pages - jnp.maximum(s, 1) * pps + u], 0, 0, 0)
        return pl.BlockSpec((None, None, PAGE_SIZE, SB_HEADS, SB_HEAD_DIM), index)

    new_spec = pl.BlockSpec((None, 1, n_new, SB_HEADS, SB_HEAD_DIM), lambda b, s, pt: (layer, b, 0, 0, 0))
    page_specs = [page_spec(u) for u in range(pps)]
    grid_spec = pltpu.PrefetchScalarGridSpec(
        num_scalar_prefetch=1,
        grid=(bn, n_pages // pps + 1),
        in_specs=[pl.BlockSpec((1, n_rows, SB_HEAD_DIM), lambda b, s, pt: (b, 0, 0)),
                  pl.BlockSpec((1, V7X_LANES), lambda b, s, pt: (0, 0)),
                  new_spec, new_spec] + page_specs + page_specs,
        out_specs=pl.BlockSpec((1, n_new, D_MODEL), lambda b, s, pt: (b, 0, 0)),
        scratch_shapes=[pltpu.VMEM((n_rows, SB_HEAD_DIM), F32),
                        pltpu.VMEM((V7X_SUBLANES, V7X_LANES), F32),
                        pltpu.VMEM((2, KEYS_PER_TILE, SB_HEADS, SB_HEAD_DIM), F32),
                        pltpu.VMEM((2 * V7X_LANES, 2 * V7X_LANES), BF16)],
    )
    return pl.pallas_call(
        functools.partial(_sb_sample_kernel, n_new=n_new, pages_per_step=pps),
        out_shape=jax.ShapeDtypeStruct((bn, n_new, D_MODEL), F32),
        grid_spec=grid_spec,
        compiler_params=_cparams(2),
        name="sb_attn_sample",
    )(page_table, q_rows, bias_l, k_new, v_new, *([cache_k] * pps), *([cache_v] * pps))


def _trunk(x3, conv_state, attend, p, *, bm, even_rows):
    bn, seq, d = x3.shape
    rows = bn * seq
    n_odd = p['w_qkv'].shape[0]
    x = x3.reshape(rows, d)
    h = rms_cast(x, p['norm_mix_pre'][0])
    conv_new, sgv_new = [], []
    kbuf = jnp.zeros((n_odd, rows, d), F32)
    vbuf = jnp.zeros((n_odd, rows, d), F32)
    for l in range(DEPTH):
        i = l // 2
        if l % 2 == 0:
            proj = matmul(h, p['w_in_ab'], i, bm=bm, bn=1024)
            y, c, vr = even_mix(proj.reshape(bn, seq, IN_AB), conv_state[i], p['conv_w'][i], p['sg_ln_g'][i],
                                p['sg_ln_b'][i], p['sg_w'][i], p['sg_b'][i], rows=even_rows)
            conv_new.append(c)
            sgv_new.append(vr)
            mixed, w_mix = y.reshape(rows, d), p['w_out_ab']
        else:
            q = matmul(h, p['w_qkv'], i, bm=bm, bn=1024, col0=0, n_cols=d)
            kbuf = matmul(h, p['w_qkv'], i, bm=bm, bn=1024, col0=d, n_cols=d, into=(kbuf, i))
            vbuf = matmul(h, p['w_qkv'], i, bm=bm, bn=1024, col0=2 * d, n_cols=d, into=(vbuf, i))
            o = attend(i, q.reshape(bn, seq, d), kbuf.reshape(n_odd, bn, seq, d), vbuf.reshape(n_odd, bn, seq, d))
            mixed, w_mix = o.reshape(rows, d), p['w_o']
        x, h = matmul_post_norm(mixed, w_mix, i, x, p['norm_mix_post'][l], p['norm_ffn_pre'][l], bm=512)
        a = swiglu_up(h, p['w_gate'], p['w_up'], l, bm=bm, bn=512)
        m = matmul(a, p['w_down'], l, bm=min(bm, 512), bn=512)
        x, h = post_norm(x, m, p['norm_ffn_post'][l], p['norm_mix_pre'][l + 1] if l + 1 < DEPTH else None)
    kv_shape = (n_odd, bn, seq, SB_HEADS, SB_HEAD_DIM)
    return (x.reshape(bn, seq, d), jnp.stack(conv_new), jnp.stack(sgv_new),
            kbuf.reshape(kv_shape), vbuf.reshape(kv_shape))


def kernel(x_prompt, x_sample, state_conv, cache_k, cache_v, page_table, norm_mix_pre, norm_mix_post,
           norm_ffn_pre, norm_ffn_post, w_in_ab, conv_w, sg_ln_g, sg_ln_b, sg_w, sg_b, w_out_ab, w_qkv,
           sb_bias, w_o, w_gate, w_up, w_down):
    p = dict(norm_mix_pre=norm_mix_pre, norm_mix_post=norm_mix_post, norm_ffn_pre=norm_ffn_pre,
             norm_ffn_post=norm_ffn_post, w_in_ab=w_in_ab, conv_w=conv_w, sg_ln_g=sg_ln_g,
             sg_ln_b=sg_ln_b, sg_w=sg_w, sg_b=sg_b, w_out_ab=w_out_ab, w_qkv=w_qkv, sb_bias=sb_bias,
             w_o=w_o, w_gate=w_gate, w_up=w_up, w_down=w_down)
    n_even = state_conv.shape[0]

    bp = x_prompt.shape[0]
    conv0 = jnp.zeros((n_even, bp, CONV_W - 1, CONV_DIM), x_prompt.dtype)

    def attend_prompt(i, q, kbuf, vbuf):
        return sb_attn_prompt(q, kbuf, vbuf, i, sb_bias[i], tq=512, tk=256)

    y_prompt, conv_p, sgv_p, k_p, v_p = _trunk(x_prompt, conv0, attend_prompt, p, bm=1024, even_rows=256)

    def attend_sample(i, q, kbuf, vbuf):
        return sb_attn_sample(q, kbuf, vbuf, i, cache_k, cache_v, page_table, sb_bias[i])

    y_sample, conv_s, sgv_s, k_s, v_s = _trunk(x_sample, state_conv, attend_sample, p, bm=1024,
                                               even_rows=256)
    return (y_prompt, y_sample, conv_p, conv_s, sgv_p, sgv_s, k_p, v_p, k_s, v_s)
```

```python
import functools

import jax
import jax.numpy as jnp
from jax import lax
from jax.experimental import pallas as pl
from jax.experimental.pallas import tpu as pltpu

D_MODEL = 2048
DEPTH = 4
PAGE_SIZE = 128
CONV_DIM = D_MODEL // 2
CONV_W = 3
SG_DIM = D_MODEL // 2
SG_HEADS = 8
SG_HEAD_DIM = SG_DIM // SG_HEADS
CHUNK = 128
SB_HEADS = 16
SB_HEAD_DIM = D_MODEL // SB_HEADS
D_FF = 5632
IN_AB = 3 * CONV_DIM + 2 * SG_DIM
EPS = 1e-6

V7X_LANES = 128
V7X_SUBLANES = 8
V7X_VMEM_LIMIT_BYTES = 56 * 1024 * 1024

BF16 = jnp.bfloat16
F32 = jnp.float32

HEADS_PER_STEP = 2
SAMPLE_PAGES_PER_STEP = 8
KEYS_PER_TILE = V7X_LANES // SB_HEADS
NT_DIMS = (((1,), (1,)), ((), ()))


def _cparams(n_axes):
    return pltpu.CompilerParams(
        dimension_semantics=("arbitrary",) * n_axes,
        vmem_limit_bytes=V7X_VMEM_LIMIT_BYTES,
    )


def _row_block(m, want):
    return want if m % want == 0 else m


def _rms_cast_kernel(x_ref, g_ref, h_ref):
    x = x_ref[...]
    y = x * lax.rsqrt(jnp.mean(x * x, axis=-1, keepdims=True) + EPS)
    h_ref[...] = (y * g_ref[...]).astype(h_ref.dtype)


def rms_cast(x, g):
    m, d = x.shape
    bm = _row_block(m, 512)
    return pl.pallas_call(
        _rms_cast_kernel,
        out_shape=jax.ShapeDtypeStruct((m, d), BF16),
        grid=(m // bm,),
        in_specs=[pl.BlockSpec((bm, d), lambda i: (i, 0)),
                  pl.BlockSpec((1, d), lambda i: (0, 0))],
        out_specs=pl.BlockSpec((bm, d), lambda i: (i, 0)),
        compiler_params=_cparams(1),
        name="rms_cast",
    )(x, g.reshape(1, d))


def _post_norm_kernel(x_ref, m_ref, gp_ref, gn_ref, xo_ref, *maybe_h_ref):
    m = m_ref[...].astype(F32)
    y = m * lax.rsqrt(jnp.mean(m * m, axis=-1, keepdims=True) + EPS)
    xn = x_ref[...] + y * gp_ref[...]
    xo_ref[...] = xn
    for h_ref in maybe_h_ref:
        hn = xn * lax.rsqrt(jnp.mean(xn * xn, axis=-1, keepdims=True) + EPS)
        h_ref[...] = (hn * gn_ref[...]).astype(h_ref.dtype)


def post_norm(x, m, g_post, g_next):
    rows, d = x.shape
    bm = _row_block(rows, 512)
    with_next = g_next is not None
    row_spec = pl.BlockSpec((bm, d), lambda i: (i, 0))
    gain_spec = pl.BlockSpec((1, d), lambda i: (0, 0))
    out_shape = [jax.ShapeDtypeStruct((rows, d), F32)]
    if with_next:
        out_shape.append(jax.ShapeDtypeStruct((rows, d), BF16))
    else:
        g_next = g_post
    out = pl.pallas_call(
        _post_norm_kernel,
        out_shape=tuple(out_shape),
        grid=(rows // bm,),
        in_specs=[row_spec, row_spec, gain_spec, gain_spec],
        out_specs=tuple([row_spec] * len(out_shape)),
        compiler_params=_cparams(1),
        name="post_norm",
    )(x, m, g_post.reshape(1, d), g_next.reshape(1, d))
    return (out[0], out[1]) if with_next else (out[0], None)


def _mm_kernel(x_ref, w_ref, *rest):
    o_ref, wb_ref = rest[-2:]

    @pl.when(pl.program_id(1) == 0)
    def _():
        wb_ref[...] = w_ref[...].astype(BF16)

    o_ref[...] = jnp.dot(x_ref[...].astype(BF16), wb_ref[...],
                         preferred_element_type=F32).astype(o_ref.dtype)


def matmul(x, w, layer, *, bm, bn, out_dtype=F32, col0=0, n_cols=None, into=None):
    m, k = x.shape
    n_cols = w.shape[2] if n_cols is None else n_cols
    bm = _row_block(m, bm)
    jb0 = col0 // bn
    in_specs = [pl.BlockSpec((bm, k), lambda j, i: (i, 0)),
                pl.BlockSpec((None, k, bn), lambda j, i: (layer, 0, jb0 + j))]
    args = [x, w]
    if into is None:
        out_shape = jax.ShapeDtypeStruct((m, n_cols), out_dtype)
        out_spec = pl.BlockSpec((bm, bn), lambda j, i: (i, j))
        aliases = {}
    else:
        buf, slot = into
        out_shape = jax.ShapeDtypeStruct(buf.shape, buf.dtype)
        out_spec = pl.BlockSpec((None, bm, bn), lambda j, i: (slot, i, j))
        in_specs.append(pl.BlockSpec(memory_space=pl.ANY))
        args.append(buf)
        aliases = {2: 0}
    return pl.pallas_call(
        _mm_kernel,
        out_shape=out_shape,
        grid=(n_cols // bn, m // bm),
        in_specs=in_specs,
        out_specs=out_spec,
        scratch_shapes=[pltpu.VMEM((k, bn), BF16)],
        input_output_aliases=aliases,
        compiler_params=_cparams(2),
        name="matmul",
    )(*args)


def _mm_post_norm_kernel(y_ref, w_ref, x_ref, gp_ref, gn_ref, xo_ref, h_ref, wb_ref):
    @pl.when(pl.program_id(0) == 0)
    def _():
        wb_ref[...] = w_ref[...].astype(BF16)

    m = jnp.dot(y_ref[...].astype(BF16), wb_ref[...], preferred_element_type=F32)
    nm = m * lax.rsqrt(jnp.mean(m * m, axis=-1, keepdims=True) + EPS)
    xn = x_ref[...] + nm * gp_ref[...]
    xo_ref[...] = xn
    hn = xn * lax.rsqrt(jnp.mean(xn * xn, axis=-1, keepdims=True) + EPS)
    h_ref[...] = (hn * gn_ref[...]).astype(h_ref.dtype)


def matmul_post_norm(y, w, layer, x, g_post, g_next, *, bm):
    rows, k = y.shape
    n = w.shape[2]
    bm = _row_block(rows, bm)
    row_spec = pl.BlockSpec((bm, n), lambda i: (i, 0))
    gain_spec = pl.BlockSpec((1, n), lambda i: (0, 0))
    return pl.pallas_call(
        _mm_post_norm_kernel,
        out_shape=(jax.ShapeDtypeStruct((rows, n), F32), jax.ShapeDtypeStruct((rows, n), BF16)),
        grid=(rows // bm,),
        in_specs=[pl.BlockSpec((bm, k), lambda i: (i, 0)),
                  pl.BlockSpec((None, k, n), lambda i: (layer, 0, 0), pipeline_mode=pl.Buffered(1)),
                  row_spec, gain_spec, gain_spec],
        out_specs=(row_spec, row_spec),
        scratch_shapes=[pltpu.VMEM((k, n), BF16)],
        compiler_params=_cparams(1),
        name="matmul_post_norm",
    )(y, w, x, g_post.reshape(1, n), g_next.reshape(1, n))


def _swiglu_up_kernel(x_ref, wg_ref, wu_ref, o_ref, wgb_ref, wub_ref):
    @pl.when(pl.program_id(1) == 0)
    def _():
        wgb_ref[...] = wg_ref[...].astype(BF16)
        wub_ref[...] = wu_ref[...].astype(BF16)

    x = x_ref[...]
    g = jnp.dot(x, wgb_ref[...], preferred_element_type=F32)
    u = jnp.dot(x, wub_ref[...], preferred_element_type=F32)
    o_ref[...] = (g * jax.nn.sigmoid(g) * u).astype(o_ref.dtype)


def swiglu_up(x, wg, wu, layer, *, bm, bn):
    m, k = x.shape
    n = wg.shape[2]
    bm = _row_block(m, bm)
    return pl.pallas_call(
        _swiglu_up_kernel,
        out_shape=jax.ShapeDtypeStruct((m, n), BF16),
        grid=(n // bn, m // bm),
        in_specs=[pl.BlockSpec((bm, k), lambda j, i: (i, 0)),
                  pl.BlockSpec((None, k, bn), lambda j, i: (layer, 0, j)),
                  pl.BlockSpec((None, k, bn), lambda j, i: (layer, 0, j))],
        out_specs=pl.BlockSpec((bm, bn), lambda j, i: (i, j)),
        scratch_shapes=[pltpu.VMEM((k, bn), BF16), pltpu.VMEM((k, bn), BF16)],
        compiler_params=_cparams(2),
        name="swiglu_up",
    )(x, wg, wu)


def _even_mix_kernel(gb_ref, gc_ref, xa_ref, u_ref, vv_ref, prefix_ref, cw_ref, lng_ref, lnb_ref,
                     sgw_ref, sgbt_ref, y_ref, cnew_ref, vlast_ref, cbuf_ref, wtri_ref, pad_ref,
                     *, rows, n_steps):
    b = pl.program_id(0)
    l = pl.program_id(1)
    carry_at = V7X_SUBLANES - (CONV_W - 1)

    @pl.when((b == 0) & (l == 0))
    def _():
        t = lax.broadcasted_iota(jnp.int32, (CHUNK, CHUNK), 0)
        s = lax.broadcasted_iota(jnp.int32, (CHUNK, CHUNK), 1)
        for h in range(SG_HEADS):
            wtri_ref[h] = jnp.where(s <= t, sgw_ref[h], 0.0).astype(BF16)

    @pl.when(l == 0)
    def _():
        cbuf_ref[carry_at:V7X_SUBLANES, :] = prefix_ref[0]

    c = gc_ref[0].astype(F32) * xa_ref[0].astype(F32)
    cbuf_ref[V7X_SUBLANES:V7X_SUBLANES + rows, :] = c
    conv = cw_ref[CONV_W - 1:CONV_W, :] * c
    for j in range(CONV_W - 1):
        conv = conv + cw_ref[j:j + 1, :] * cbuf_ref[carry_at + j:carry_at + j + rows, :]
    y_ref[0, :, 0:CONV_DIM] = (gb_ref[0].astype(F32) * conv).astype(y_ref.dtype)
    new_hist = cbuf_ref[V7X_SUBLANES + rows - (CONV_W - 1):V7X_SUBLANES + rows, :]
    cbuf_ref[carry_at:V7X_SUBLANES, :] = new_hist

    @pl.when(l == n_steps - 1)
    def _():
        cnew_ref[0] = new_hist

    vv = vv_ref[0].astype(F32)
    mu = jnp.mean(vv, axis=-1, keepdims=True)
    xc = vv - mu
    vn = xc * lax.rsqrt(jnp.mean(xc * xc, axis=-1, keepdims=True) + EPS) * lng_ref[...] + lnb_ref[...]

    n_chunks = -(-rows // CHUNK)
    last_rows = rows - (n_chunks - 1) * CHUNK

    @pl.when(l == n_steps - 1)
    def _():
        vlast_ref[0] = vn[(n_chunks - 1) * CHUNK:, :]

    if rows % CHUNK != 0:
        pad_ref[...] = jnp.zeros_like(pad_ref)
        pad_ref[0:rows, :] = vn
        vn_full = pad_ref[...]
    else:
        vn_full = vn
    vnb = vn_full.astype(BF16)
    u = u_ref[0].astype(F32)
    for ci in range(n_chunks):
        r0 = ci * CHUNK
        nr = CHUNK if ci < n_chunks - 1 else last_rows
        for h in range(SG_HEADS):
            c0 = h * SG_HEAD_DIM
            s = jnp.dot(wtri_ref[h], vnb[r0:r0 + CHUNK, c0:c0 + SG_HEAD_DIM], preferred_element_type=F32)
            s = s + sgbt_ref[:, h:h + 1]
            yb = u[r0:r0 + nr, c0:c0 + SG_HEAD_DIM] * s[0:nr]
            y_ref[0, r0:r0 + nr, CONV_DIM + c0:CONV_DIM + c0 + SG_HEAD_DIM] = yb.astype(y_ref.dtype)


def even_mix(proj, prefix, conv_w, ln_g, ln_b, sg_w, sg_b, *, rows):
    bn, seq, _ = proj.shape
    rows = min(rows, seq)
    n_steps = seq // rows
    r_last = seq - ((seq - 1) // CHUNK) * CHUNK
    pad_rows = -(-rows // CHUNK) * CHUNK

    def col(cb):
        return pl.BlockSpec((1, rows, CONV_DIM), lambda b, l: (b, l, cb))

    const2 = lambda b, l: (0, 0)
    y_dtype = BF16 if rows % 16 == 0 else F32
    return pl.pallas_call(
        functools.partial(_even_mix_kernel, rows=rows, n_steps=n_steps),
        out_shape=(jax.ShapeDtypeStruct((bn, seq, D_MODEL), y_dtype),
                   jax.ShapeDtypeStruct((bn, CONV_W - 1, CONV_DIM), F32),
                   jax.ShapeDtypeStruct((bn, r_last, SG_DIM), F32)),
        grid=(bn, n_steps),
        in_specs=[col(0), col(1), col(2), col(3), col(4),
                  pl.BlockSpec((1, CONV_W - 1, CONV_DIM), lambda b, l: (b, 0, 0)),
                  pl.BlockSpec((CONV_W, CONV_DIM), const2),
                  pl.BlockSpec((1, SG_DIM), const2),
                  pl.BlockSpec((1, SG_DIM), const2),
                  pl.BlockSpec((SG_HEADS, CHUNK, CHUNK), lambda b, l: (0, 0, 0)),
                  pl.BlockSpec((CHUNK, SG_HEADS), const2)],
        out_specs=(pl.BlockSpec((1, rows, D_MODEL), lambda b, l: (b, l, 0)),
                   pl.BlockSpec((1, CONV_W - 1, CONV_DIM), lambda b, l: (b, 0, 0)),
                   pl.BlockSpec((1, r_last, SG_DIM), lambda b, l: (b, 0, 0))),
        scratch_shapes=[pltpu.VMEM((V7X_SUBLANES + rows, CONV_DIM), F32),
                        pltpu.VMEM((SG_HEADS, CHUNK, CHUNK), BF16),
                        pltpu.VMEM((pad_rows, SG_DIM), F32)],
        compiler_params=_cparams(2),
        name="even_mix",
    )(proj, proj, proj, proj, proj, prefix, conv_w, ln_g.reshape(1, SG_DIM), ln_b.reshape(1, SG_DIM),
      sg_w, sg_b.T)


def _softplus_terms(z):
    sp = jnp.maximum(z, 0.0) + jnp.log(1.0 + jnp.exp(-jnp.abs(z)))
    return sp, z - sp


def _split_bf16(x):
    hi = x.astype(BF16)
    lo = (x - hi.astype(F32)).astype(BF16)
    return hi, lo


def _sb_prompt_kernel(bias_ref, q_ref, k_ref, v_ref, o_ref, kb_ref, vb_ref, later_ref, *, tq, tk):
    b = pl.program_id(0)
    hp = pl.program_id(1)
    i = pl.program_id(2)
    blocks_per_tile = tq // tk

    @pl.when((b == 0) & (hp == 0) & (i == 0))
    def _():
        row = lax.broadcasted_iota(jnp.int32, (tk, tk), 0)
        col = lax.broadcasted_iota(jnp.int32, (tk, tk), 1)
        later = jnp.where(row > col, 1.0, 0.0).astype(BF16)
        later_ref[0:tk, :] = later
        later_ref[tk:2 * tk, :] = later

    @pl.when(i == 0)
    def _():
        kb_ref[...] = k_ref[0].astype(BF16)
        vb_ref[...] = v_ref[0].astype(BF16)

    scale = SB_HEAD_DIM ** -0.5
    q = q_ref[0].astype(BF16)

    heads = range(HEADS_PER_STEP)

    def block(j, carry, first_row):
        lanes = [slice(g * SB_HEAD_DIM, (g + 1) * SB_HEAD_DIM) for g in heads]
        start = pl.multiple_of(j * tk, tk)
        r0 = 0 if first_row is None else first_row
        acc = [carry[2 * g][r0:] for g in heads]
        run = [carry[2 * g + 1][r0:] for g in heads]
        z = [lax.dot_general(q[r0:, lanes[g]], kb_ref[pl.ds(start, tk), lanes[g]], NT_DIMS,
                             preferred_element_type=F32) * scale + bias_ref[hp * HEADS_PER_STEP + g]
             for g in heads]
        terms = [_softplus_terms(z[g]) for g in heads]
        sp = [terms[g][0] for g in heads]
        log_beta = [terms[g][1] for g in heads]
        if first_row is not None:
            row = lax.broadcasted_iota(jnp.int32, (tq - r0, tk), 0)
            col = lax.broadcasted_iota(jnp.int32, (tq - r0, tk), 1)
            visible = col < row
            sp = [jnp.where(visible, sp[g], 0.0) for g in heads]
        split = [_split_bf16(sp[g]) for g in heads]
        behind = [jnp.dot(jnp.concatenate(split[g], axis=1), later_ref[...], preferred_element_type=F32)
                  for g in heads]
        p = [jnp.exp(log_beta[g] - behind[g] - run[g]) for g in heads]
        if first_row is not None:
            p = [jnp.where(visible, p[g], 0.0) for g in heads]
        out = []
        for g in heads:
            vs = vb_ref[pl.ds(start, tk), lanes[g]]
            acc_g = acc[g] + jnp.dot(p[g].astype(BF16), vs, preferred_element_type=F32)
            run_g = run[g] + (behind[g][:, 0:1] + sp[g][:, 0:1])
            if r0:
                acc_g = jnp.concatenate([carry[2 * g][:r0], acc_g], axis=0)
                run_g = jnp.concatenate([carry[2 * g + 1][:r0], run_g], axis=0)
            out.extend((acc_g, run_g))
        return tuple(out)

    carry = tuple(c for _ in heads for c in (jnp.zeros((tq, SB_HEAD_DIM), F32), jnp.zeros((tq, 1), F32)))
    for d in reversed(range(blocks_per_tile)):
        carry = block(i * blocks_per_tile + d, carry, d * tk)
    carry = lax.fori_loop(0, i * blocks_per_tile,
                          lambda step, c: block(i * blocks_per_tile - 1 - step, c, None), carry)
    for g in heads:
        o_ref[0, :, g * SB_HEAD_DIM:(g + 1) * SB_HEAD_DIM] = carry[2 * g].astype(o_ref.dtype)


def sb_attn_prompt(q, kbuf, vbuf, layer, bias, *, tq, tk):
    bn, seq, _ = q.shape
    nq = seq // tq
    width = HEADS_PER_STEP * SB_HEAD_DIM
    kv_spec = pl.BlockSpec((None, 1, seq, width), lambda b, hp, i: (layer, b, 0, hp))
    return pl.pallas_call(
        functools.partial(_sb_prompt_kernel, tq=tq, tk=tk),
        out_shape=jax.ShapeDtypeStruct((bn, seq, D_MODEL), BF16),
        grid=(bn, SB_HEADS // HEADS_PER_STEP, nq),
        in_specs=[pl.BlockSpec(memory_space=pltpu.SMEM),
                  pl.BlockSpec((1, tq, width), lambda b, hp, i: (b, i, hp)),
                  kv_spec, kv_spec],
        out_specs=pl.BlockSpec((1, tq, width), lambda b, hp, i: (b, i, hp)),
        scratch_shapes=[pltpu.VMEM((seq, width), BF16), pltpu.VMEM((seq, width), BF16),
                        pltpu.VMEM((2 * tk, tk), BF16)],
        compiler_params=_cparams(3),
        name="sb_attn_prompt",
    )(bias, q, kbuf, vbuf)


def _head_of(pair_index):
    return pair_index & (SB_HEADS - 1)


def _key_of(pair_index):
    return pair_index >> (SB_HEADS.bit_length() - 1)


def _sb_sample_kernel(pt_ref, q_ref, bias_ref, knew_ref, vnew_ref, *rest, n_new, pages_per_step):
    kpg_refs = rest[0:pages_per_step]
    vpg_refs = rest[pages_per_step:2 * pages_per_step]
    o_ref, acc_ref, run_ref, new_ref, scanw_ref = rest[2 * pages_per_step:]
    b = pl.program_id(0)
    s = pl.program_id(1)
    n_steps = pl.num_programs(1)
    scale = SB_HEAD_DIM ** -0.5
    n_rows = n_new * SB_HEADS

    @pl.when((b == 0) & (s == 0))
    def _():
        r = lax.broadcasted_iota(jnp.int32, (V7X_LANES, 2 * V7X_LANES), 0)
        c = lax.broadcasted_iota(jnp.int32, (V7X_LANES, 2 * V7X_LANES), 1)
        same_head = _head_of(r) == _head_of(c)
        after = _key_of(r) > _key_of(c & (V7X_LANES - 1))
        w = jnp.where(same_head & (after | (c >= V7X_LANES)), 1.0, 0.0).astype(BF16)
        scanw_ref[0:V7X_LANES, :] = w
        scanw_ref[V7X_LANES:2 * V7X_LANES, :] = w

    head_row = lax.broadcasted_iota(jnp.int32, (SB_HEADS, V7X_LANES), 0)
    head_lane = _head_of(lax.broadcasted_iota(jnp.int32, (SB_HEADS, V7X_LANES), 1))
    own = head_row == head_lane
    query_row = lax.broadcasted_iota(jnp.int32, (V7X_SUBLANES, V7X_LANES), 0)

    def absorb(k2d, v2d, n_tiles, visible, acc, run):
        zfull = lax.dot_general(q_ref[0], k2d, NT_DIMS, preferred_element_type=F32)
        tiles = []
        for t in range(n_tiles):
            lanes = slice(t * V7X_LANES, (t + 1) * V7X_LANES)
            tile = jnp.zeros((V7X_SUBLANES, V7X_LANES), F32)
            for i in range(n_new):
                own_head = jnp.sum(jnp.where(own, zfull[i * SB_HEADS:(i + 1) * SB_HEADS, lanes], 0.0),
                                   axis=0, keepdims=True)
                tile = jnp.where(query_row == i, own_head, tile)
            tiles.append(tile)
        z = jnp.concatenate(tiles, axis=0) * scale + bias_ref[...]
        sp, log_beta = _softplus_terms(z)
        if visible is not None:
            sp = jnp.where(visible, sp, 0.0)
        hi, lo = _split_bf16(sp)
        scan = jnp.dot(jnp.concatenate([hi, lo], axis=1), scanw_ref[...], preferred_element_type=F32)
        p_tiles = [None] * n_tiles
        for t in reversed(range(n_tiles)):
            rs = slice(t * V7X_SUBLANES, (t + 1) * V7X_SUBLANES)
            p_t = jnp.exp(log_beta[rs] - scan[rs, 0:V7X_LANES] - run)
            if visible is not None:
                p_t = jnp.where(visible[rs], p_t, 0.0)
            p_tiles[t] = p_t
            run = run + scan[rs, V7X_LANES:2 * V7X_LANES]
        blocks = []
        for i in range(n_new):
            blocks.append(jnp.concatenate(
                [jnp.where(own, jnp.broadcast_to(p_tiles[t][i:i + 1, :], (SB_HEADS, V7X_LANES)), 0.0)
                 for t in range(n_tiles)], axis=1).astype(BF16))
        pbd = jnp.concatenate(blocks, axis=0)
        return acc + jnp.dot(pbd, v2d, preferred_element_type=F32), run

    @pl.when(s == 0)
    def _():
        new_ref[...] = jnp.zeros_like(new_ref)
        new_ref[0, 0:n_new] = knew_ref[0]
        new_ref[1, 0:n_new] = vnew_ref[0]
        key = _key_of(lax.broadcasted_iota(jnp.int32, (V7X_SUBLANES, V7X_LANES), 1))
        acc, run = absorb(new_ref[0].reshape(V7X_LANES, SB_HEAD_DIM).astype(BF16),
                          new_ref[1].reshape(V7X_LANES, SB_HEAD_DIM).astype(BF16), 1, key < query_row,
                          jnp.zeros(acc_ref.shape, F32), jnp.zeros(run_ref.shape, F32))
        acc_ref[...] = acc
        run_ref[...] = run

    @pl.when(s > 0)
    def _():
        n_rows_page = PAGE_SIZE * SB_HEADS
        acc, run = acc_ref[...], run_ref[...]
        for u in reversed(range(pages_per_step)):
            acc, run = absorb(kpg_refs[u][...].reshape(n_rows_page, SB_HEAD_DIM).astype(BF16),
                              vpg_refs[u][...].reshape(n_rows_page, SB_HEAD_DIM).astype(BF16),
                              n_rows_page // V7X_LANES, None, acc, run)
        acc_ref[...] = acc
        run_ref[...] = run

    @pl.when(s == n_steps - 1)
    def _():
        for i in range(n_new):
            for h in range(SB_HEADS):
                r = i * SB_HEADS + h
                o_ref[0, i:i + 1, h * SB_HEAD_DIM:(h + 1) * SB_HEAD_DIM] = acc_ref[r:r + 1, :]


def sb_attn_sample(q, kbuf, vbuf, layer, cache_k, cache_v, page_table, bias):
    bn, n_new, _ = q.shape
    assert n_new <= V7X_SUBLANES and KEYS_PER_TILE >= n_new
    n_layers = kbuf.shape[0]
    n_pages = page_table.shape[1]
    n_rows = n_new * SB_HEADS
    q_rows = q.reshape(bn, n_rows, SB_HEAD_DIM).astype(BF16)
    bias_l = jnp.tile(bias, KEYS_PER_TILE).reshape(1, V7X_LANES)
    k_new = kbuf.reshape(n_layers, bn, n_new, SB_HEADS, SB_HEAD_DIM)
    v_new = vbuf.reshape(n_layers, bn, n_new, SB_HEADS, SB_HEAD_DIM)

    pps = next(c for c in (SAMPLE_PAGES_PER_STEP, 2, 1) if n_pages % c == 0)

    def page_spec(u):
        def index(b, s, pt):
            return (layer, pt[b, n_pages - jnp.maximum(s, 1) * pps + u], 0, 0, 0)
        return pl.BlockSpec((None, None, PAGE_SIZE, SB_HEADS, SB_HEAD_DIM), index)

    new_spec = pl.BlockSpec((None, 1, n_new, SB_HEADS, SB_HEAD_DIM), lambda b, s, pt: (layer, b, 0, 0, 0))
    page_specs = [page_spec(u) for u in range(pps)]
    grid_spec = pltpu.PrefetchScalarGridSpec(
        num_scalar_prefetch=1,
        grid=(bn, n_pages // pps + 1),
        in_specs=[pl.BlockSpec((1, n_rows, SB_HEAD_DIM), lambda b, s, pt: (b, 0, 0)),
                  pl.BlockSpec((1, V7X_LANES), lambda b, s, pt: (0, 0)),
                  new_spec, new_spec] + page_specs + page_specs,
        out_specs=pl.BlockSpec((1, n_new, D_MODEL), lambda b, s, pt: (b, 0, 0)),
        scratch_shapes=[pltpu.VMEM((n_rows, SB_HEAD_DIM), F32),
                        pltpu.VMEM((V7X_SUBLANES, V7X_LANES), F32),
                        pltpu.VMEM((2, KEYS_PER_TILE, SB_HEADS, SB_HEAD_DIM), F32),
                        pltpu.VMEM((2 * V7X_LANES, 2 * V7X_LANES), BF16)],
    )
    return pl.pallas_call(
        functools.partial(_sb_sample_kernel, n_new=n_new, pages_per_step=pps),
        out_shape=jax.ShapeDtypeStruct((bn, n_new, D_MODEL), F32),
        grid_spec=grid_spec,
        compiler_params=_cparams(2),
        name="sb_attn_sample",
    )(page_table, q_rows, bias_l, k_new, v_new, *([cache_k] * pps), *([cache_v] * pps))


def _trunk(x3, conv_state, attend, p, *, bm, even_rows):
    bn, seq, d = x3.shape
    rows = bn * seq
    n_odd = p['w_qkv'].shape[0]
    x = x3.reshape(rows, d)
    h = rms_cast(x, p['norm_mix_pre'][0])
    conv_new, sgv_new = [], []
    kbuf = jnp.zeros((n_odd, rows, d), F32)
    vbuf = jnp.zeros((n_odd, rows, d), F32)
    for l in range(DEPTH):
        i = l // 2
        if l % 2 == 0:
            proj = matmul(h, p['w_in_ab'], i, bm=bm, bn=1024, out_dtype=BF16 if seq % 16 == 0 else F32)
            y, c, vr = even_mix(proj.reshape(bn, seq, IN_AB), conv_state[i], p['conv_w'][i], p['sg_ln_g'][i],
                                p['sg_ln_b'][i], p['sg_w'][i], p['sg_b'][i], rows=even_rows)
            conv_new.append(c)
            sgv_new.append(vr)
            mixed, w_mix = y.reshape(rows, d), p['w_out_ab']
        else:
            q = matmul(h, p['w_qkv'], i, bm=bm, bn=1024, col0=0, n_cols=d, out_dtype=BF16)
            kbuf = matmul(h, p['w_qkv'], i, bm=bm, bn=1024, col0=d, n_cols=d, into=(kbuf, i))
            vbuf = matmul(h, p['w_qkv'], i, bm=bm, bn=1024, col0=2 * d, n_cols=d, into=(vbuf, i))
            o = attend(i, q.reshape(bn, seq, d), kbuf.reshape(n_odd, bn, seq, d), vbuf.reshape(n_odd, bn, seq, d))
            mixed, w_mix = o.reshape(rows, d), p['w_o']
        x, h = matmul_post_norm(mixed, w_mix, i, x, p['norm_mix_post'][l], p['norm_ffn_pre'][l], bm=512)
        a = swiglu_up(h, p['w_gate'], p['w_up'], l, bm=bm, bn=512)
        m = matmul(a, p['w_down'], l, bm=min(bm, 512), bn=512, out_dtype=BF16)
        x, h = post_norm(x, m, p['norm_ffn_post'][l], p['norm_mix_pre'][l + 1] if l + 1 < DEPTH else None)
    kv_shape = (n_odd, bn, seq, SB_HEADS, SB_HEAD_DIM)
    return (x.reshape(bn, seq, d), jnp.stack(conv_new), jnp.stack(sgv_new),
            kbuf.reshape(kv_shape), vbuf.reshape(kv_shape))


def kernel(x_prompt, x_sample, state_conv, cache_k, cache_v, page_table, norm_mix_pre, norm_mix_post,
           norm_ffn_pre, norm_ffn_post, w_in_ab, conv_w, sg_ln_g, sg_ln_b, sg_w, sg_b, w_out_ab, w_qkv,
           sb_bias, w_o, w_gate, w_up, w_down):
    p = dict(norm_mix_pre=norm_mix_pre, norm_mix_post=norm_mix_post, norm_ffn_pre=norm_ffn_pre,
             norm_ffn_post=norm_ffn_post, w_in_ab=w_in_ab, conv_w=conv_w, sg_ln_g=sg_ln_g,
             sg_ln_b=sg_ln_b, sg_w=sg_w, sg_b=sg_b, w_out_ab=w_out_ab, w_qkv=w_qkv, sb_bias=sb_bias,
             w_o=w_o, w_gate=w_gate, w_up=w_up, w_down=w_down)
    n_even = state_conv.shape[0]

    bp = x_prompt.shape[0]
    conv0 = jnp.zeros((n_even, bp, CONV_W - 1, CONV_DIM), x_prompt.dtype)

    def attend_prompt(i, q, kbuf, vbuf):
        return sb_attn_prompt(q, kbuf, vbuf, i, sb_bias[i], tq=512, tk=256)

    y_prompt, conv_p, sgv_p, k_p, v_p = _trunk(x_prompt, conv0, attend_prompt, p, bm=1024, even_rows=256)

    def attend_sample(i, q, kbuf, vbuf):
        return sb_attn_sample(q, kbuf, vbuf, i, cache_k, cache_v, page_table, sb_bias[i])

    y_sample, conv_s, sgv_s, k_s, v_s = _trunk(x_sample, state_conv, attend_sample, p, bm=1024,
                                               even_rows=256)
    return (y_prompt, y_sample, conv_p, conv_s, sgv_p, sgv_s, k_p, v_p, k_s, v_s)
```

```python
import functools

import jax
import jax.numpy as jnp
from jax import lax
from jax.experimental import pallas as pl
from jax.experimental.pallas import tpu as pltpu

D_MODEL = 2048
DEPTH = 4
PAGE_SIZE = 128
CONV_DIM = D_MODEL // 2
CONV_W = 3
SG_DIM = D_MODEL // 2
SG_HEADS = 8
SG_HEAD_DIM = SG_DIM // SG_HEADS
CHUNK = 128
SB_HEADS = 16
SB_HEAD_DIM = D_MODEL // SB_HEADS
D_FF = 5632
IN_AB = 3 * CONV_DIM + 2 * SG_DIM
EPS = 1e-6

V7X_LANES = 128
V7X_SUBLANES = 8
V7X_VMEM_LIMIT_BYTES = 56 * 1024 * 1024

BF16 = jnp.bfloat16
F32 = jnp.float32

HEADS_PER_STEP = 2
SAMPLE_PAGES_PER_STEP = 8
KEYS_PER_TILE = V7X_LANES // SB_HEADS
NT_DIMS = (((1,), (1,)), ((), ()))


def _cparams(n_axes):
    return pltpu.CompilerParams(
        dimension_semantics=("arbitrary",) * n_axes,
        vmem_limit_bytes=V7X_VMEM_LIMIT_BYTES,
    )


def _row_block(m, want):
    return want if m % want == 0 else m


def _rms_cast_kernel(x_ref, g_ref, h_ref):
    x = x_ref[...]
    y = x * lax.rsqrt(jnp.mean(x * x, axis=-1, keepdims=True) + EPS)
    h_ref[...] = (y * g_ref[...]).astype(h_ref.dtype)


def rms_cast(x, g):
    m, d = x.shape
    bm = _row_block(m, 512)
    return pl.pallas_call(
        _rms_cast_kernel,
        out_shape=jax.ShapeDtypeStruct((m, d), BF16),
        grid=(m // bm,),
        in_specs=[pl.BlockSpec((bm, d), lambda i: (i, 0)),
                  pl.BlockSpec((1, d), lambda i: (0, 0))],
        out_specs=pl.BlockSpec((bm, d), lambda i: (i, 0)),
        compiler_params=_cparams(1),
        name="rms_cast",
    )(x, g.reshape(1, d))


def _post_norm_kernel(x_ref, m_ref, gp_ref, gn_ref, xo_ref, *maybe_h_ref):
    m = m_ref[...].astype(F32)
    y = m * lax.rsqrt(jnp.mean(m * m, axis=-1, keepdims=True) + EPS)
    xn = x_ref[...] + y * gp_ref[...]
    xo_ref[...] = xn
    for h_ref in maybe_h_ref:
        hn = xn * lax.rsqrt(jnp.mean(xn * xn, axis=-1, keepdims=True) + EPS)
        h_ref[...] = (hn * gn_ref[...]).astype(h_ref.dtype)


def post_norm(x, m, g_post, g_next):
    rows, d = x.shape
    bm = _row_block(rows, 512)
    with_next = g_next is not None
    row_spec = pl.BlockSpec((bm, d), lambda i: (i, 0))
    gain_spec = pl.BlockSpec((1, d), lambda i: (0, 0))
    out_shape = [jax.ShapeDtypeStruct((rows, d), F32)]
    if with_next:
        out_shape.append(jax.ShapeDtypeStruct((rows, d), BF16))
    else:
        g_next = g_post
    out = pl.pallas_call(
        _post_norm_kernel,
        out_shape=tuple(out_shape),
        grid=(rows // bm,),
        in_specs=[row_spec, row_spec, gain_spec, gain_spec],
        out_specs=tuple([row_spec] * len(out_shape)),
        compiler_params=_cparams(1),
        name="post_norm",
    )(x, m, g_post.reshape(1, d), g_next.reshape(1, d))
    return (out[0], out[1]) if with_next else (out[0], None)


def _on_group(row_step, n_main, fn, main_refs, tail_refs):
    pl.when(row_step < n_main)(lambda: fn(*main_refs))
    pl.when(row_step == n_main)(lambda: fn(*tail_refs))


def _mm_kernel(x1_ref, x2_ref, w_ref, *rest, n_main):
    o1_ref, o2_ref, wb_ref = rest[-3:]
    i = pl.program_id(1)

    @pl.when(i == 0)
    def _():
        wb_ref[...] = w_ref[...].astype(BF16)

    def product(x_ref, o_ref):
        o_ref[...] = jnp.dot(x_ref[...].astype(BF16), wb_ref[...],
                             preferred_element_type=F32).astype(o_ref.dtype)

    _on_group(i, n_main, product, (x1_ref, o1_ref), (x2_ref, o2_ref))


def matmul(xs, w, layer, *, bm, bn, out_dtypes=(F32, F32), col0=0, n_cols=None, into=None):
    (m1, k), (m2, _) = xs[0].shape, xs[1].shape
    n_cols = w.shape[2] if n_cols is None else n_cols
    n_main = m1 // bm
    jb0 = col0 // bn
    main_row = lambda i: jnp.minimum(i, n_main - 1)
    in_specs = [pl.BlockSpec((bm, k), lambda j, i: (main_row(i), 0)),
                pl.BlockSpec((m2, k), lambda j, i: (0, 0)),
                pl.BlockSpec((None, k, bn), lambda j, i: (layer, 0, jb0 + j))]
    args = [xs[0], xs[1], w]
    if into is None:
        out_shape = (jax.ShapeDtypeStruct((m1, n_cols), out_dtypes[0]),
                     jax.ShapeDtypeStruct((m2, n_cols), out_dtypes[1]))
        out_specs = (pl.BlockSpec((bm, bn), lambda j, i: (main_row(i), j)),
                     pl.BlockSpec((m2, bn), lambda j, i: (0, j)))
        aliases = {}
    else:
        bufs, slot = into
        out_shape = tuple(jax.ShapeDtypeStruct(b.shape, b.dtype) for b in bufs)
        out_specs = (pl.BlockSpec((None, bm, bn), lambda j, i: (slot, main_row(i), j)),
                     pl.BlockSpec((None, m2, bn), lambda j, i: (slot, 0, j)))
        in_specs += [pl.BlockSpec(memory_space=pl.ANY)] * 2
        args += list(bufs)
        aliases = {3: 0, 4: 1}
    return pl.pallas_call(
        functools.partial(_mm_kernel, n_main=n_main),
        out_shape=out_shape,
        grid=(n_cols // bn, n_main + 1),
        in_specs=in_specs,
        out_specs=out_specs,
        scratch_shapes=[pltpu.VMEM((k, bn), BF16)],
        input_output_aliases=aliases,
        compiler_params=_cparams(2),
        name="matmul",
    )(*args)


def _mm_post_norm_kernel(y1_ref, y2_ref, w_ref, x1_ref, x2_ref, gp_ref, gn_ref,
                         xo1_ref, h1_ref, xo2_ref, h2_ref, wb_ref, *, n_main):
    i = pl.program_id(0)

    @pl.when(i == 0)
    def _():
        wb_ref[...] = w_ref[...].astype(BF16)

    def project(y_ref, x_ref, xo_ref, h_ref):
        m = jnp.dot(y_ref[...].astype(BF16), wb_ref[...], preferred_element_type=F32)
        nm = m * lax.rsqrt(jnp.mean(m * m, axis=-1, keepdims=True) + EPS)
        xn = x_ref[...] + nm * gp_ref[...]
        xo_ref[...] = xn
        hn = xn * lax.rsqrt(jnp.mean(xn * xn, axis=-1, keepdims=True) + EPS)
        h_ref[...] = (hn * gn_ref[...]).astype(h_ref.dtype)

    _on_group(i, n_main, project, (y1_ref, x1_ref, xo1_ref, h1_ref), (y2_ref, x2_ref, xo2_ref, h2_ref))


def matmul_post_norm(ys, w, layer, xs, g_post, g_next, *, bm):
    (m1, k), (m2, _) = ys[0].shape, ys[1].shape
    n = w.shape[2]
    n_main = m1 // bm
    main_row = lambda i: jnp.minimum(i, n_main - 1)
    main_in = pl.BlockSpec((bm, k), lambda i: (main_row(i), 0))
    main_row_spec = pl.BlockSpec((bm, n), lambda i: (main_row(i), 0))
    tail_in = pl.BlockSpec((m2, k), lambda i: (0, 0))
    tail_row_spec = pl.BlockSpec((m2, n), lambda i: (0, 0))
    gain_spec = pl.BlockSpec((1, n), lambda i: (0, 0))
    xo1, h1, xo2, h2 = pl.pallas_call(
        functools.partial(_mm_post_norm_kernel, n_main=n_main),
        out_shape=(jax.ShapeDtypeStruct((m1, n), F32), jax.ShapeDtypeStruct((m1, n), BF16),
                   jax.ShapeDtypeStruct((m2, n), F32), jax.ShapeDtypeStruct((m2, n), BF16)),
        grid=(n_main + 1,),
        in_specs=[main_in, tail_in,
                  pl.BlockSpec((None, k, n), lambda i: (layer, 0, 0), pipeline_mode=pl.Buffered(1)),
                  main_row_spec, tail_row_spec, gain_spec, gain_spec],
        out_specs=(main_row_spec, main_row_spec, tail_row_spec, tail_row_spec),
        scratch_shapes=[pltpu.VMEM((k, n), BF16)],
        compiler_params=_cparams(1),
        name="matmul_post_norm",
    )(ys[0], ys[1], w, xs[0], xs[1], g_post.reshape(1, n), g_next.reshape(1, n))
    return (xo1, xo2), (h1, h2)


def _swiglu_up_kernel(x1_ref, x2_ref, wg_ref, wu_ref, o1_ref, o2_ref, wgb_ref, wub_ref, *, n_main):
    i = pl.program_id(1)

    @pl.when(i == 0)
    def _():
        wgb_ref[...] = wg_ref[...].astype(BF16)
        wub_ref[...] = wu_ref[...].astype(BF16)

    def gated(x_ref, o_ref):
        x = x_ref[...]
        g = jnp.dot(x, wgb_ref[...], preferred_element_type=F32)
        u = jnp.dot(x, wub_ref[...], preferred_element_type=F32)
        o_ref[...] = (g * jax.nn.sigmoid(g) * u).astype(o_ref.dtype)

    _on_group(i, n_main, gated, (x1_ref, o1_ref), (x2_ref, o2_ref))


def swiglu_up(xs, wg, wu, layer, *, bm, bn):
    (m1, k), (m2, _) = xs[0].shape, xs[1].shape
    n = wg.shape[2]
    n_main = m1 // bm
    main_row = lambda i: jnp.minimum(i, n_main - 1)
    w_spec = pl.BlockSpec((None, k, bn), lambda j, i: (layer, 0, j))
    return pl.pallas_call(
        functools.partial(_swiglu_up_kernel, n_main=n_main),
        out_shape=(jax.ShapeDtypeStruct((m1, n), BF16), jax.ShapeDtypeStruct((m2, n), BF16)),
        grid=(n // bn, n_main + 1),
        in_specs=[pl.BlockSpec((bm, k), lambda j, i: (main_row(i), 0)),
                  pl.BlockSpec((m2, k), lambda j, i: (0, 0)),
                  w_spec, w_spec],
        out_specs=(pl.BlockSpec((bm, bn), lambda j, i: (main_row(i), j)),
                   pl.BlockSpec((m2, bn), lambda j, i: (0, j))),
        scratch_shapes=[pltpu.VMEM((k, bn), BF16), pltpu.VMEM((k, bn), BF16)],
        compiler_params=_cparams(2),
        name="swiglu_up",
    )(xs[0], xs[1], wg, wu)


def _even_mix_kernel(gb_ref, gc_ref, xa_ref, u_ref, vv_ref, prefix_ref, cw_ref, lng_ref, lnb_ref,
                     sgw_ref, sgbt_ref, y_ref, cnew_ref, vlast_ref, cbuf_ref, wtri_ref, pad_ref,
                     *, rows, n_steps):
    b = pl.program_id(0)
    l = pl.program_id(1)
    carry_at = V7X_SUBLANES - (CONV_W - 1)

    @pl.when((b == 0) & (l == 0))
    def _():
        t = lax.broadcasted_iota(jnp.int32, (CHUNK, CHUNK), 0)
        s = lax.broadcasted_iota(jnp.int32, (CHUNK, CHUNK), 1)
        for h in range(SG_HEADS):
            wtri_ref[h] = jnp.where(s <= t, sgw_ref[h], 0.0).astype(BF16)

    @pl.when(l == 0)
    def _():
        cbuf_ref[carry_at:V7X_SUBLANES, :] = prefix_ref[0]

    c = gc_ref[0].astype(F32) * xa_ref[0].astype(F32)
    cbuf_ref[V7X_SUBLANES:V7X_SUBLANES + rows, :] = c
    conv = cw_ref[CONV_W - 1:CONV_W, :] * c
    for j in range(CONV_W - 1):
        conv = conv + cw_ref[j:j + 1, :] * cbuf_ref[carry_at + j:carry_at + j + rows, :]
    y_ref[0, :, 0:CONV_DIM] = (gb_ref[0].astype(F32) * conv).astype(y_ref.dtype)
    new_hist = cbuf_ref[V7X_SUBLANES + rows - (CONV_W - 1):V7X_SUBLANES + rows, :]
    cbuf_ref[carry_at:V7X_SUBLANES, :] = new_hist

    @pl.when(l == n_steps - 1)
    def _():
        cnew_ref[0] = new_hist

    vv = vv_ref[0].astype(F32)
    mu = jnp.mean(vv, axis=-1, keepdims=True)
    xc = vv - mu
    vn = xc * lax.rsqrt(jnp.mean(xc * xc, axis=-1, keepdims=True) + EPS) * lng_ref[...] + lnb_ref[...]

    n_chunks = -(-rows // CHUNK)
    last_rows = rows - (n_chunks - 1) * CHUNK

    @pl.when(l == n_steps - 1)
    def _():
        vlast_ref[0] = vn[(n_chunks - 1) * CHUNK:, :]

    if rows % CHUNK != 0:
        pad_ref[...] = jnp.zeros_like(pad_ref)
        pad_ref[0:rows, :] = vn
        vn_full = pad_ref[...]
    else:
        vn_full = vn
    vnb = vn_full.astype(BF16)
    u = u_ref[0].astype(F32)
    for ci in range(n_chunks):
        r0 = ci * CHUNK
        nr = CHUNK if ci < n_chunks - 1 else last_rows
        for h in range(SG_HEADS):
            c0 = h * SG_HEAD_DIM
            s = jnp.dot(wtri_ref[h], vnb[r0:r0 + CHUNK, c0:c0 + SG_HEAD_DIM], preferred_element_type=F32)
            s = s + sgbt_ref[:, h:h + 1]
            yb = u[r0:r0 + nr, c0:c0 + SG_HEAD_DIM] * s[0:nr]
            y_ref[0, r0:r0 + nr, CONV_DIM + c0:CONV_DIM + c0 + SG_HEAD_DIM] = yb.astype(y_ref.dtype)


def even_mix(proj, prefix, conv_w, ln_g, ln_b, sg_w, sg_b, *, rows):
    bn, seq, _ = proj.shape
    rows = min(rows, seq)
    n_steps = seq // rows
    r_last = seq - ((seq - 1) // CHUNK) * CHUNK
    pad_rows = -(-rows // CHUNK) * CHUNK

    def col(cb):
        return pl.BlockSpec((1, rows, CONV_DIM), lambda b, l: (b, l, cb))

    const2 = lambda b, l: (0, 0)
    y_dtype = BF16 if rows % 16 == 0 else F32
    return pl.pallas_call(
        functools.partial(_even_mix_kernel, rows=rows, n_steps=n_steps),
        out_shape=(jax.ShapeDtypeStruct((bn, seq, D_MODEL), y_dtype),
                   jax.ShapeDtypeStruct((bn, CONV_W - 1, CONV_DIM), F32),
                   jax.ShapeDtypeStruct((bn, r_last, SG_DIM), F32)),
        grid=(bn, n_steps),
        in_specs=[col(0), col(1), col(2), col(3), col(4),
                  pl.BlockSpec((1, CONV_W - 1, CONV_DIM), lambda b, l: (b, 0, 0)),
                  pl.BlockSpec((CONV_W, CONV_DIM), const2),
                  pl.BlockSpec((1, SG_DIM), const2),
                  pl.BlockSpec((1, SG_DIM), const2),
                  pl.BlockSpec((SG_HEADS, CHUNK, CHUNK), lambda b, l: (0, 0, 0)),
                  pl.BlockSpec((CHUNK, SG_HEADS), const2)],
        out_specs=(pl.BlockSpec((1, rows, D_MODEL), lambda b, l: (b, l, 0)),
                   pl.BlockSpec((1, CONV_W - 1, CONV_DIM), lambda b, l: (b, 0, 0)),
                   pl.BlockSpec((1, r_last, SG_DIM), lambda b, l: (b, 0, 0))),
        scratch_shapes=[pltpu.VMEM((V7X_SUBLANES + rows, CONV_DIM), F32),
                        pltpu.VMEM((SG_HEADS, CHUNK, CHUNK), BF16),
                        pltpu.VMEM((pad_rows, SG_DIM), F32)],
        compiler_params=_cparams(2),
        name="even_mix",
    )(proj, proj, proj, proj, proj, prefix, conv_w, ln_g.reshape(1, SG_DIM), ln_b.reshape(1, SG_DIM),
      sg_w, sg_b.T)


def _softplus_terms(z):
    sp = jnp.maximum(z, 0.0) + jnp.log(1.0 + jnp.exp(-jnp.abs(z)))
    return sp, z - sp


def _split_bf16(x):
    hi = x.astype(BF16)
    lo = (x - hi.astype(F32)).astype(BF16)
    return hi, lo


def _sb_prompt_kernel(bias_ref, q_ref, k_ref, v_ref, o_ref, kb_ref, vb_ref, later_ref, *, tq, tk):
    b = pl.program_id(0)
    hp = pl.program_id(1)
    i = pl.program_id(2)
    blocks_per_tile = tq // tk

    @pl.when((b == 0) & (hp == 0) & (i == 0))
    def _():
        row = lax.broadcasted_iota(jnp.int32, (tk, tk), 0)
        col = lax.broadcasted_iota(jnp.int32, (tk, tk), 1)
        later = jnp.where(row > col, 1.0, 0.0).astype(BF16)
        later_ref[0:tk, :] = later
        later_ref[tk:2 * tk, :] = later

    @pl.when(i == 0)
    def _():
        kb_ref[...] = k_ref[0].astype(BF16)
        vb_ref[...] = v_ref[0].astype(BF16)

    scale = SB_HEAD_DIM ** -0.5
    q = q_ref[0].astype(BF16)

    heads = range(HEADS_PER_STEP)

    def block(j, carry, first_row):
        lanes = [slice(g * SB_HEAD_DIM, (g + 1) * SB_HEAD_DIM) for g in heads]
        start = pl.multiple_of(j * tk, tk)
        r0 = 0 if first_row is None else first_row
        acc = [carry[2 * g][r0:] for g in heads]
        run = [carry[2 * g + 1][r0:] for g in heads]
        z = [lax.dot_general(q[r0:, lanes[g]], kb_ref[pl.ds(start, tk), lanes[g]], NT_DIMS,
                             preferred_element_type=F32) * scale + bias_ref[hp * HEADS_PER_STEP + g]
             for g in heads]
        terms = [_softplus_terms(z[g]) for g in heads]
        sp = [terms[g][0] for g in heads]
        log_beta = [terms[g][1] for g in heads]
        if first_row is not None:
            row = lax.broadcasted_iota(jnp.int32, (tq - r0, tk), 0)
            col = lax.broadcasted_iota(jnp.int32, (tq - r0, tk), 1)
            visible = col < row
            sp = [jnp.where(visible, sp[g], 0.0) for g in heads]
        split = [_split_bf16(sp[g]) for g in heads]
        behind = [jnp.dot(jnp.concatenate(split[g], axis=1), later_ref[...], preferred_element_type=F32)
                  for g in heads]
        p = [jnp.exp(log_beta[g] - behind[g] - run[g]) for g in heads]
        if first_row is not None:
            p = [jnp.where(visible, p[g], 0.0) for g in heads]
        out = []
        for g in heads:
            vs = vb_ref[pl.ds(start, tk), lanes[g]]
            acc_g = acc[g] + jnp.dot(p[g].astype(BF16), vs, preferred_element_type=F32)
            run_g = run[g] + (behind[g][:, 0:1] + sp[g][:, 0:1])
            if r0:
                acc_g = jnp.concatenate([carry[2 * g][:r0], acc_g], axis=0)
                run_g = jnp.concatenate([carry[2 * g + 1][:r0], run_g], axis=0)
            out.extend((acc_g, run_g))
        return tuple(out)

    carry = tuple(c for _ in heads for c in (jnp.zeros((tq, SB_HEAD_DIM), F32), jnp.zeros((tq, 1), F32)))
    for d in reversed(range(blocks_per_tile)):
        carry = block(i * blocks_per_tile + d, carry, d * tk)
    carry = lax.fori_loop(0, i * blocks_per_tile,
                          lambda step, c: block(i * blocks_per_tile - 1 - step, c, None), carry)
    for g in heads:
        o_ref[0, :, g * SB_HEAD_DIM:(g + 1) * SB_HEAD_DIM] = carry[2 * g].astype(o_ref.dtype)


def sb_attn_prompt(q, kbuf, vbuf, layer, bias, *, tq, tk):
    bn, seq, _ = q.shape
    nq = seq // tq
    width = HEADS_PER_STEP * SB_HEAD_DIM
    kv_spec = pl.BlockSpec((None, 1, seq, width), lambda b, hp, i: (layer, b, 0, hp))
    return pl.pallas_call(
        functools.partial(_sb_prompt_kernel, tq=tq, tk=tk),
        out_shape=jax.ShapeDtypeStruct((bn, seq, D_MODEL), BF16),
        grid=(bn, SB_HEADS // HEADS_PER_STEP, nq),
        in_specs=[pl.BlockSpec(memory_space=pltpu.SMEM),
                  pl.BlockSpec((1, tq, width), lambda b, hp, i: (b, i, hp)),
                  kv_spec, kv_spec],
        out_specs=pl.BlockSpec((1, tq, width), lambda b, hp, i: (b, i, hp)),
        scratch_shapes=[pltpu.VMEM((seq, width), BF16), pltpu.VMEM((seq, width), BF16),
                        pltpu.VMEM((2 * tk, tk), BF16)],
        compiler_params=_cparams(3),
        name="sb_attn_prompt",
    )(bias, q, kbuf, vbuf)


def _head_of(pair_index):
    return pair_index & (SB_HEADS - 1)


def _key_of(pair_index):
    return pair_index >> (SB_HEADS.bit_length() - 1)


def _sb_sample_kernel(pt_ref, q_ref, bias_ref, knew_ref, vnew_ref, *rest, n_new, pages_per_step):
    kpg_refs = rest[0:pages_per_step]
    vpg_refs = rest[pages_per_step:2 * pages_per_step]
    o_ref, acc_ref, run_ref, new_ref, scanw_ref = rest[2 * pages_per_step:]
    b = pl.program_id(0)
    s = pl.program_id(1)
    n_steps = pl.num_programs(1)
    scale = SB_HEAD_DIM ** -0.5
    n_rows = n_new * SB_HEADS

    @pl.when((b == 0) & (s == 0))
    def _():
        r = lax.broadcasted_iota(jnp.int32, (V7X_LANES, 2 * V7X_LANES), 0)
        c = lax.broadcasted_iota(jnp.int32, (V7X_LANES, 2 * V7X_LANES), 1)
        same_head = _head_of(r) == _head_of(c)
        after = _key_of(r) > _key_of(c & (V7X_LANES - 1))
        w = jnp.where(same_head & (after | (c >= V7X_LANES)), 1.0, 0.0).astype(BF16)
        scanw_ref[0:V7X_LANES, :] = w
        scanw_ref[V7X_LANES:2 * V7X_LANES, :] = w

    head_row = lax.broadcasted_iota(jnp.int32, (SB_HEADS, V7X_LANES), 0)
    head_lane = _head_of(lax.broadcasted_iota(jnp.int32, (SB_HEADS, V7X_LANES), 1))
    own = head_row == head_lane
    query_row = lax.broadcasted_iota(jnp.int32, (V7X_SUBLANES, V7X_LANES), 0)

    def absorb(k2d, v2d, n_tiles, visible, acc, run):
        zfull = lax.dot_general(q_ref[0], k2d, NT_DIMS, preferred_element_type=F32)
        tiles = []
        for t in range(n_tiles):
            lanes = slice(t * V7X_LANES, (t + 1) * V7X_LANES)
            tile = jnp.zeros((V7X_SUBLANES, V7X_LANES), F32)
            for i in range(n_new):
                own_head = jnp.sum(jnp.where(own, zfull[i * SB_HEADS:(i + 1) * SB_HEADS, lanes], 0.0),
                                   axis=0, keepdims=True)
                tile = jnp.where(query_row == i, own_head, tile)
            tiles.append(tile)
        z = jnp.concatenate(tiles, axis=0) * scale + bias_ref[...]
        sp, log_beta = _softplus_terms(z)
        if visible is not None:
            sp = jnp.where(visible, sp, 0.0)
        hi, lo = _split_bf16(sp)
        scan = jnp.dot(jnp.concatenate([hi, lo], axis=1), scanw_ref[...], preferred_element_type=F32)
        p_tiles = [None] * n_tiles
        for t in reversed(range(n_tiles)):
            rs = slice(t * V7X_SUBLANES, (t + 1) * V7X_SUBLANES)
            p_t = jnp.exp(log_beta[rs] - scan[rs, 0:V7X_LANES] - run)
            if visible is not None:
                p_t = jnp.where(visible[rs], p_t, 0.0)
            p_tiles[t] = p_t
            run = run + scan[rs, V7X_LANES:2 * V7X_LANES]
        blocks = []
        for i in range(n_new):
            blocks.append(jnp.concatenate(
                [jnp.where(own, jnp.broadcast_to(p_tiles[t][i:i + 1, :], (SB_HEADS, V7X_LANES)), 0.0)
                 for t in range(n_tiles)], axis=1).astype(BF16))
        pbd = jnp.concatenate(blocks, axis=0)
        return acc + jnp.dot(pbd, v2d, preferred_element_type=F32), run

    @pl.when(s == 0)
    def _():
        new_ref[...] = jnp.zeros_like(new_ref)
        new_ref[0, 0:n_new] = knew_ref[0]
        new_ref[1, 0:n_new] = vnew_ref[0]
        key = _key_of(lax.broadcasted_iota(jnp.int32, (V7X_SUBLANES, V7X_LANES), 1))
        acc, run = absorb(new_ref[0].reshape(V7X_LANES, SB_HEAD_DIM).astype(BF16),
                          new_ref[1].reshape(V7X_LANES, SB_HEAD_DIM).astype(BF16), 1, key < query_row,
                          jnp.zeros(acc_ref.shape, F32), jnp.zeros(run_ref.shape, F32))
        acc_ref[...] = acc
        run_ref[...] = run

    @pl.when(s > 0)
    def _():
        n_rows_page = PAGE_SIZE * SB_HEADS
        acc, run = acc_ref[...], run_ref[...]
        for u in reversed(range(pages_per_step)):
            acc, run = absorb(kpg_refs[u][...].reshape(n_rows_page, SB_HEAD_DIM).astype(BF16),
                              vpg_refs[u][...].reshape(n_rows_page, SB_HEAD_DIM).astype(BF16),
                              n_rows_page // V7X_LANES, None, acc, run)
        acc_ref[...] = acc
        run_ref[...] = run

    @pl.when(s == n_steps - 1)
    def _():
        for i in range(n_new):
            for h in range(SB_HEADS):
                r = i * SB_HEADS + h
                o_ref[0, i:i + 1, h * SB_HEAD_DIM:(h + 1) * SB_HEAD_DIM] = acc_ref[r:r + 1, :]


def sb_attn_sample(q, kbuf, vbuf, layer, cache_k, cache_v, page_table, bias):
    bn, n_new, _ = q.shape
    assert n_new <= V7X_SUBLANES and KEYS_PER_TILE >= n_new
    n_layers = kbuf.shape[0]
    n_pages = page_table.shape[1]
    n_rows = n_new * SB_HEADS
    q_rows = q.reshape(bn, n_rows, SB_HEAD_DIM).astype(BF16)
    bias_l = jnp.tile(bias, KEYS_PER_TILE).reshape(1, V7X_LANES)
    k_new = kbuf.reshape(n_layers, bn, n_new, SB_HEADS, SB_HEAD_DIM)
    v_new = vbuf.reshape(n_layers, bn, n_new, SB_HEADS, SB_HEAD_DIM)

    pps = next(c for c in (SAMPLE_PAGES_PER_STEP, 2, 1) if n_pages % c == 0)

    def page_spec(u):
        def index(b, s, pt):
            return (layer, pt[b, n_pages - jnp.maximum(s, 1) * pps + u], 0, 0, 0)
        return pl.BlockSpec((None, None, PAGE_SIZE, SB_HEADS, SB_HEAD_DIM), index)

    new_spec = pl.BlockSpec((None, 1, n_new, SB_HEADS, SB_HEAD_DIM), lambda b, s, pt: (layer, b, 0, 0, 0))
    page_specs = [page_spec(u) for u in range(pps)]
    grid_spec = pltpu.PrefetchScalarGridSpec(
        num_scalar_prefetch=1,
        grid=(bn, n_pages // pps + 1),
        in_specs=[pl.BlockSpec((1, n_rows, SB_HEAD_DIM), lambda b, s, pt: (b, 0, 0)),
                  pl.BlockSpec((1, V7X_LANES), lambda b, s, pt: (0, 0)),
                  new_spec, new_spec] + page_specs + page_specs,
        out_specs=pl.BlockSpec((1, n_new, D_MODEL), lambda b, s, pt: (b, 0, 0)),
        scratch_shapes=[pltpu.VMEM((n_rows, SB_HEAD_DIM), F32),
                        pltpu.VMEM((V7X_SUBLANES, V7X_LANES), F32),
                        pltpu.VMEM((2, KEYS_PER_TILE, SB_HEADS, SB_HEAD_DIM), F32),
                        pltpu.VMEM((2 * V7X_LANES, 2 * V7X_LANES), BF16)],
    )
    return pl.pallas_call(
        functools.partial(_sb_sample_kernel, n_new=n_new, pages_per_step=pps),
        out_shape=jax.ShapeDtypeStruct((bn, n_new, D_MODEL), F32),
        grid_spec=grid_spec,
        compiler_params=_cparams(2),
        name="sb_attn_sample",
    )(page_table, q_rows, bias_l, k_new, v_new, *([cache_k] * pps), *([cache_v] * pps))


def _trunks(x3s, conv_states, attends, p, *, bm, even_rows):
    groups = range(len(x3s))
    shapes = [x3.shape for x3 in x3s]
    d = D_MODEL
    rows = [bn * seq for bn, seq, _ in shapes]
    n_odd = p['w_qkv'].shape[0]
    xs = [x3.reshape(rows[g], d) for g, x3 in enumerate(x3s)]
    hs = [rms_cast(x, p['norm_mix_pre'][0]) for x in xs]
    conv_new = [[] for _ in groups]
    sgv_new = [[] for _ in groups]
    kbufs = tuple(jnp.zeros((n_odd, rows[g], d), F32) for g in groups)
    vbufs = tuple(jnp.zeros((n_odd, rows[g], d), F32) for g in groups)
    for l in range(DEPTH):
        i = l // 2
        if l % 2 == 0:
            proj_dtypes = tuple(BF16 if seq % 16 == 0 else F32 for _, seq, _ in shapes)
            projs = matmul(hs, p['w_in_ab'], i, bm=bm, bn=1024, out_dtypes=proj_dtypes)
            mixed = []
            for g in groups:
                bn, seq, _ = shapes[g]
                y, c, vr = even_mix(projs[g].reshape(bn, seq, IN_AB), conv_states[g][i], p['conv_w'][i],
                                    p['sg_ln_g'][i], p['sg_ln_b'][i], p['sg_w'][i], p['sg_b'][i],
                                    rows=even_rows)
                conv_new[g].append(c)
                sgv_new[g].append(vr)
                mixed.append(y.reshape(rows[g], d))
            w_mix = p['w_out_ab']
        else:
            qs = matmul(hs, p['w_qkv'], i, bm=bm, bn=1024, col0=0, n_cols=d, out_dtypes=(BF16, BF16))
            kbufs = matmul(hs, p['w_qkv'], i, bm=bm, bn=1024, col0=d, n_cols=d, into=(kbufs, i))
            vbufs = matmul(hs, p['w_qkv'], i, bm=bm, bn=1024, col0=2 * d, n_cols=d, into=(vbufs, i))
            mixed = []
            for g in groups:
                bn, seq, _ = shapes[g]
                o = attends[g](i, qs[g].reshape(bn, seq, d), kbufs[g].reshape(n_odd, bn, seq, d),
                               vbufs[g].reshape(n_odd, bn, seq, d))
                mixed.append(o.reshape(rows[g], d))
            w_mix = p['w_o']
        xs, hs = matmul_post_norm(mixed, w_mix, i, xs, p['norm_mix_post'][l], p['norm_ffn_pre'][l], bm=512)
        a = swiglu_up(hs, p['w_gate'], p['w_up'], l, bm=bm, bn=512)
        ms = matmul(a, p['w_down'], l, bm=512, bn=512, out_dtypes=(BF16, BF16))
        g_next = p['norm_mix_pre'][l + 1] if l + 1 < DEPTH else None
        xs, hs = zip(*[post_norm(xs[g], ms[g], p['norm_ffn_post'][l], g_next) for g in groups])
    out = []
    for g in groups:
        bn, seq, _ = shapes[g]
        kv_shape = (n_odd, bn, seq, SB_HEADS, SB_HEAD_DIM)
        out.append((xs[g].reshape(bn, seq, d), jnp.stack(conv_new[g]), jnp.stack(sgv_new[g]),
                    kbufs[g].reshape(kv_shape), vbufs[g].reshape(kv_shape)))
    return out


def kernel(x_prompt, x_sample, state_conv, cache_k, cache_v, page_table, norm_mix_pre, norm_mix_post,
           norm_ffn_pre, norm_ffn_post, w_in_ab, conv_w, sg_ln_g, sg_ln_b, sg_w, sg_b, w_out_ab, w_qkv,
           sb_bias, w_o, w_gate, w_up, w_down):
    p = dict(norm_mix_pre=norm_mix_pre, norm_mix_post=norm_mix_post, norm_ffn_pre=norm_ffn_pre,
             norm_ffn_post=norm_ffn_post, w_in_ab=w_in_ab, conv_w=conv_w, sg_ln_g=sg_ln_g,
             sg_ln_b=sg_ln_b, sg_w=sg_w, sg_b=sg_b, w_out_ab=w_out_ab, w_qkv=w_qkv, sb_bias=sb_bias,
             w_o=w_o, w_gate=w_gate, w_up=w_up, w_down=w_down)
    n_even = state_conv.shape[0]

    bp = x_prompt.shape[0]
    conv0 = jnp.zeros((n_even, bp, CONV_W - 1, CONV_DIM), x_prompt.dtype)

    def attend_prompt(i, q, kbuf, vbuf):
        return sb_attn_prompt(q, kbuf, vbuf, i, sb_bias[i], tq=512, tk=256)

    def attend_sample(i, q, kbuf, vbuf):
        return sb_attn_sample(q, kbuf, vbuf, i, cache_k, cache_v, page_table, sb_bias[i])

    (y_prompt, conv_p, sgv_p, k_p, v_p), (y_sample, conv_s, sgv_s, k_s, v_s) = _trunks(
        (x_prompt, x_sample), (conv0, state_conv), (attend_prompt, attend_sample), p, bm=1024, even_rows=256)
    return (y_prompt, y_sample, conv_p, conv_s, sgv_p, sgv_s, k_p, v_p, k_s, v_s)
```

```python
import functools

import jax
import jax.numpy as jnp
from jax import lax
from jax.experimental import pallas as pl
from jax.experimental.pallas import tpu as pltpu

D_MODEL = 2048
DEPTH = 4
PAGE_SIZE = 128
CONV_DIM = D_MODEL // 2
CONV_W = 3
SG_DIM = D_MODEL // 2
SG_HEADS = 8
SG_HEAD_DIM = SG_DIM // SG_HEADS
CHUNK = 128
SB_HEADS = 16
SB_HEAD_DIM = D_MODEL // SB_HEADS
D_FF = 5632
IN_AB = 3 * CONV_DIM + 2 * SG_DIM
EPS = 1e-6

V7X_LANES = 128
V7X_SUBLANES = 8
V7X_VMEM_LIMIT_BYTES = 56 * 1024 * 1024

BF16 = jnp.bfloat16
F32 = jnp.float32

HEADS_PER_STEP = 2
SAMPLE_PAGES_PER_STEP = 8
KEYS_PER_TILE = V7X_LANES // SB_HEADS
NT_DIMS = (((1,), (1,)), ((), ()))


def _cparams(n_axes):
    return pltpu.CompilerParams(
        dimension_semantics=("arbitrary",) * n_axes,
        vmem_limit_bytes=V7X_VMEM_LIMIT_BYTES,
    )


def _row_block(m, want):
    return want if m % want == 0 else m


def _rms_cast_kernel(x_ref, g_ref, h_ref):
    x = x_ref[...]
    y = x * lax.rsqrt(jnp.mean(x * x, axis=-1, keepdims=True) + EPS)
    h_ref[...] = (y * g_ref[...]).astype(h_ref.dtype)


def rms_cast(x, g):
    m, d = x.shape
    bm = _row_block(m, 512)
    return pl.pallas_call(
        _rms_cast_kernel,
        out_shape=jax.ShapeDtypeStruct((m, d), BF16),
        grid=(m // bm,),
        in_specs=[pl.BlockSpec((bm, d), lambda i: (i, 0)),
                  pl.BlockSpec((1, d), lambda i: (0, 0))],
        out_specs=pl.BlockSpec((bm, d), lambda i: (i, 0)),
        compiler_params=_cparams(1),
        name="rms_cast",
    )(x, g.reshape(1, d))


def _post_norm_kernel(x_ref, m_ref, gp_ref, gn_ref, xo_ref, *maybe_h_ref):
    m = m_ref[...].astype(F32)
    y = m * lax.rsqrt(jnp.mean(m * m, axis=-1, keepdims=True) + EPS)
    xn = x_ref[...] + y * gp_ref[...]
    xo_ref[...] = xn
    for h_ref in maybe_h_ref:
        hn = xn * lax.rsqrt(jnp.mean(xn * xn, axis=-1, keepdims=True) + EPS)
        h_ref[...] = (hn * gn_ref[...]).astype(h_ref.dtype)


def post_norm(x, m, g_post, g_next):
    rows, d = x.shape
    bm = _row_block(rows, 512)
    with_next = g_next is not None
    row_spec = pl.BlockSpec((bm, d), lambda i: (i, 0))
    gain_spec = pl.BlockSpec((1, d), lambda i: (0, 0))
    out_shape = [jax.ShapeDtypeStruct((rows, d), F32)]
    if with_next:
        out_shape.append(jax.ShapeDtypeStruct((rows, d), BF16))
    else:
        g_next = g_post
    out = pl.pallas_call(
        _post_norm_kernel,
        out_shape=tuple(out_shape),
        grid=(rows // bm,),
        in_specs=[row_spec, row_spec, gain_spec, gain_spec],
        out_specs=tuple([row_spec] * len(out_shape)),
        compiler_params=_cparams(1),
        name="post_norm",
    )(x, m, g_post.reshape(1, d), g_next.reshape(1, d))
    return (out[0], out[1]) if with_next else (out[0], None)


def _on_group(row_step, fn, main_refs, small_refs):
    pl.when(row_step == 0)(lambda: fn(*small_refs))
    pl.when(row_step > 0)(lambda: fn(*main_refs))


def _main_row(row_step):
    return jnp.maximum(row_step - 1, 0)


def _mm_kernel(x1_ref, x2_ref, w_ref, *rest):
    o1_ref, o2_ref, wb_ref = rest[-3:]
    i = pl.program_id(1)

    @pl.when(i == 0)
    def _():
        wb_ref[...] = w_ref[...].astype(BF16)

    def product(x_ref, o_ref):
        o_ref[...] = jnp.dot(x_ref[...].astype(BF16), wb_ref[...],
                             preferred_element_type=F32).astype(o_ref.dtype)

    _on_group(i, product, (x1_ref, o1_ref), (x2_ref, o2_ref))


def matmul(xs, w, layer, *, bm, bn, out_dtypes=(F32, F32), col0=0, n_cols=None, into=None):
    (m1, k), (m2, _) = xs[0].shape, xs[1].shape
    n_cols = w.shape[2] if n_cols is None else n_cols
    n_main = m1 // bm
    jb0 = col0 // bn
    main_row = _main_row
    in_specs = [pl.BlockSpec((bm, k), lambda j, i: (main_row(i), 0)),
                pl.BlockSpec((m2, k), lambda j, i: (0, 0)),
                pl.BlockSpec((None, k, bn), lambda j, i: (layer, 0, jb0 + j))]
    args = [xs[0], xs[1], w]
    if into is None:
        out_shape = (jax.ShapeDtypeStruct((m1, n_cols), out_dtypes[0]),
                     jax.ShapeDtypeStruct((m2, n_cols), out_dtypes[1]))
        out_specs = (pl.BlockSpec((bm, bn), lambda j, i: (main_row(i), j)),
                     pl.BlockSpec((m2, bn), lambda j, i: (0, j)))
        aliases = {}
    else:
        bufs, slot = into
        out_shape = tuple(jax.ShapeDtypeStruct(b.shape, b.dtype) for b in bufs)
        out_specs = (pl.BlockSpec((None, bm, bn), lambda j, i: (slot, main_row(i), j)),
                     pl.BlockSpec((None, m2, bn), lambda j, i: (slot, 0, j)))
        in_specs += [pl.BlockSpec(memory_space=pl.ANY)] * 2
        args += list(bufs)
        aliases = {3: 0, 4: 1}
    return pl.pallas_call(
        _mm_kernel,
        out_shape=out_shape,
        grid=(n_cols // bn, n_main + 1),
        in_specs=in_specs,
        out_specs=out_specs,
        scratch_shapes=[pltpu.VMEM((k, bn), BF16)],
        input_output_aliases=aliases,
        compiler_params=_cparams(2),
        name="matmul",
    )(*args)


def _mm_post_norm_kernel(y1_ref, y2_ref, w_ref, x1_ref, x2_ref, gp_ref, gn_ref,
                         xo1_ref, h1_ref, xo2_ref, h2_ref, wb_ref):
    i = pl.program_id(0)

    @pl.when(i == 0)
    def _():
        wb_ref[...] = w_ref[...].astype(BF16)

    def project(y_ref, x_ref, xo_ref, h_ref):
        m = jnp.dot(y_ref[...].astype(BF16), wb_ref[...], preferred_element_type=F32)
        nm = m * lax.rsqrt(jnp.mean(m * m, axis=-1, keepdims=True) + EPS)
        xn = x_ref[...] + nm * gp_ref[...]
        xo_ref[...] = xn
        hn = xn * lax.rsqrt(jnp.mean(xn * xn, axis=-1, keepdims=True) + EPS)
        h_ref[...] = (hn * gn_ref[...]).astype(h_ref.dtype)

    _on_group(i, project, (y1_ref, x1_ref, xo1_ref, h1_ref), (y2_ref, x2_ref, xo2_ref, h2_ref))


def matmul_post_norm(ys, w, layer, xs, g_post, g_next, *, bm):
    (m1, k), (m2, _) = ys[0].shape, ys[1].shape
    n = w.shape[2]
    n_main = m1 // bm
    main_in = pl.BlockSpec((bm, k), lambda i: (_main_row(i), 0))
    main_row_spec = pl.BlockSpec((bm, n), lambda i: (_main_row(i), 0))
    tail_in = pl.BlockSpec((m2, k), lambda i: (0, 0))
    tail_row_spec = pl.BlockSpec((m2, n), lambda i: (0, 0))
    gain_spec = pl.BlockSpec((1, n), lambda i: (0, 0))
    xo1, h1, xo2, h2 = pl.pallas_call(
        _mm_post_norm_kernel,
        out_shape=(jax.ShapeDtypeStruct((m1, n), F32), jax.ShapeDtypeStruct((m1, n), BF16),
                   jax.ShapeDtypeStruct((m2, n), F32), jax.ShapeDtypeStruct((m2, n), BF16)),
        grid=(n_main + 1,),
        in_specs=[main_in, tail_in,
                  pl.BlockSpec((None, k, n), lambda i: (layer, 0, 0), pipeline_mode=pl.Buffered(1)),
                  main_row_spec, tail_row_spec, gain_spec, gain_spec],
        out_specs=(main_row_spec, main_row_spec, tail_row_spec, tail_row_spec),
        scratch_shapes=[pltpu.VMEM((k, n), BF16)],
        compiler_params=_cparams(1),
        name="matmul_post_norm",
    )(ys[0], ys[1], w, xs[0], xs[1], g_post.reshape(1, n), g_next.reshape(1, n))
    return (xo1, xo2), (h1, h2)


def _swiglu_up_kernel(x1_ref, x2_ref, wg_ref, wu_ref, o1_ref, o2_ref, wgb_ref, wub_ref):
    i = pl.program_id(1)

    @pl.when(i == 0)
    def _():
        wgb_ref[...] = wg_ref[...].astype(BF16)
        wub_ref[...] = wu_ref[...].astype(BF16)

    def gated(x_ref, o_ref):
        x = x_ref[...]
        g = jnp.dot(x, wgb_ref[...], preferred_element_type=F32)
        u = jnp.dot(x, wub_ref[...], preferred_element_type=F32)
        o_ref[...] = (g * jax.nn.sigmoid(g) * u).astype(o_ref.dtype)

    _on_group(i, gated, (x1_ref, o1_ref), (x2_ref, o2_ref))


def swiglu_up(xs, wg, wu, layer, *, bm, bn):
    (m1, k), (m2, _) = xs[0].shape, xs[1].shape
    n = wg.shape[2]
    n_main = m1 // bm
    main_row = _main_row
    w_spec = pl.BlockSpec((None, k, bn), lambda j, i: (layer, 0, j))
    return pl.pallas_call(
        _swiglu_up_kernel,
        out_shape=(jax.ShapeDtypeStruct((m1, n), BF16), jax.ShapeDtypeStruct((m2, n), BF16)),
        grid=(n // bn, n_main + 1),
        in_specs=[pl.BlockSpec((bm, k), lambda j, i: (main_row(i), 0)),
                  pl.BlockSpec((m2, k), lambda j, i: (0, 0)),
                  w_spec, w_spec],
        out_specs=(pl.BlockSpec((bm, bn), lambda j, i: (main_row(i), j)),
                   pl.BlockSpec((m2, bn), lambda j, i: (0, j))),
        scratch_shapes=[pltpu.VMEM((k, bn), BF16), pltpu.VMEM((k, bn), BF16)],
        compiler_params=_cparams(2),
        name="swiglu_up",
    )(xs[0], xs[1], wg, wu)


def _even_mix_kernel(gb_ref, gc_ref, xa_ref, u_ref, vv_ref, prefix_ref, cw_ref, lng_ref, lnb_ref,
                     sgw_ref, sgbt_ref, y_ref, cnew_ref, vlast_ref, cbuf_ref, wtri_ref, pad_ref,
                     *, rows, n_steps):
    b = pl.program_id(0)
    l = pl.program_id(1)
    carry_at = V7X_SUBLANES - (CONV_W - 1)

    @pl.when((b == 0) & (l == 0))
    def _():
        t = lax.broadcasted_iota(jnp.int32, (CHUNK, CHUNK), 0)
        s = lax.broadcasted_iota(jnp.int32, (CHUNK, CHUNK), 1)
        for h in range(SG_HEADS):
            wtri_ref[h] = jnp.where(s <= t, sgw_ref[h], 0.0).astype(BF16)

    @pl.when(l == 0)
    def _():
        cbuf_ref[carry_at:V7X_SUBLANES, :] = prefix_ref[0]

    c = gc_ref[0].astype(F32) * xa_ref[0].astype(F32)
    cbuf_ref[V7X_SUBLANES:V7X_SUBLANES + rows, :] = c
    conv = cw_ref[CONV_W - 1:CONV_W, :] * c
    for j in range(CONV_W - 1):
        conv = conv + cw_ref[j:j + 1, :] * cbuf_ref[carry_at + j:carry_at + j + rows, :]
    y_ref[0, :, 0:CONV_DIM] = (gb_ref[0].astype(F32) * conv).astype(y_ref.dtype)
    new_hist = cbuf_ref[V7X_SUBLANES + rows - (CONV_W - 1):V7X_SUBLANES + rows, :]
    cbuf_ref[carry_at:V7X_SUBLANES, :] = new_hist

    @pl.when(l == n_steps - 1)
    def _():
        cnew_ref[0] = new_hist

    vv = vv_ref[0].astype(F32)
    mu = jnp.mean(vv, axis=-1, keepdims=True)
    xc = vv - mu
    vn = xc * lax.rsqrt(jnp.mean(xc * xc, axis=-1, keepdims=True) + EPS) * lng_ref[...] + lnb_ref[...]

    n_chunks = -(-rows // CHUNK)
    last_rows = rows - (n_chunks - 1) * CHUNK

    @pl.when(l == n_steps - 1)
    def _():
        vlast_ref[0] = vn[(n_chunks - 1) * CHUNK:, :]

    if rows % CHUNK != 0:
        pad_ref[...] = jnp.zeros_like(pad_ref)
        pad_ref[0:rows, :] = vn
        vn_full = pad_ref[...]
    else:
        vn_full = vn
    vnb = vn_full.astype(BF16)
    u = u_ref[0].astype(F32)
    for ci in range(n_chunks):
        r0 = ci * CHUNK
        nr = CHUNK if ci < n_chunks - 1 else last_rows
        for h in range(SG_HEADS):
            c0 = h * SG_HEAD_DIM
            s = jnp.dot(wtri_ref[h], vnb[r0:r0 + CHUNK, c0:c0 + SG_HEAD_DIM], preferred_element_type=F32)
            s = s + sgbt_ref[:, h:h + 1]
            yb = u[r0:r0 + nr, c0:c0 + SG_HEAD_DIM] * s[0:nr]
            y_ref[0, r0:r0 + nr, CONV_DIM + c0:CONV_DIM + c0 + SG_HEAD_DIM] = yb.astype(y_ref.dtype)


def even_mix(proj, prefix, conv_w, ln_g, ln_b, sg_w, sg_b, *, rows):
    bn, seq, _ = proj.shape
    rows = min(rows, seq)
    n_steps = seq // rows
    r_last = seq - ((seq - 1) // CHUNK) * CHUNK
    pad_rows = -(-rows // CHUNK) * CHUNK

    def col(cb):
        return pl.BlockSpec((1, rows, CONV_DIM), lambda b, l: (b, l, cb))

    const2 = lambda b, l: (0, 0)
    y_dtype = BF16 if rows % 16 == 0 else F32
    return pl.pallas_call(
        functools.partial(_even_mix_kernel, rows=rows, n_steps=n_steps),
        out_shape=(jax.ShapeDtypeStruct((bn, seq, D_MODEL), y_dtype),
                   jax.ShapeDtypeStruct((bn, CONV_W - 1, CONV_DIM), F32),
                   jax.ShapeDtypeStruct((bn, r_last, SG_DIM), F32)),
        grid=(bn, n_steps),
        in_specs=[col(0), col(1), col(2), col(3), col(4),
                  pl.BlockSpec((1, CONV_W - 1, CONV_DIM), lambda b, l: (b, 0, 0)),
                  pl.BlockSpec((CONV_W, CONV_DIM), const2),
                  pl.BlockSpec((1, SG_DIM), const2),
                  pl.BlockSpec((1, SG_DIM), const2),
                  pl.BlockSpec((SG_HEADS, CHUNK, CHUNK), lambda b, l: (0, 0, 0)),
                  pl.BlockSpec((CHUNK, SG_HEADS), const2)],
        out_specs=(pl.BlockSpec((1, rows, D_MODEL), lambda b, l: (b, l, 0)),
                   pl.BlockSpec((1, CONV_W - 1, CONV_DIM), lambda b, l: (b, 0, 0)),
                   pl.BlockSpec((1, r_last, SG_DIM), lambda b, l: (b, 0, 0))),
        scratch_shapes=[pltpu.VMEM((V7X_SUBLANES + rows, CONV_DIM), F32),
                        pltpu.VMEM((SG_HEADS, CHUNK, CHUNK), BF16),
                        pltpu.VMEM((pad_rows, SG_DIM), F32)],
        compiler_params=_cparams(2),
        name="even_mix",
    )(proj, proj, proj, proj, proj, prefix, conv_w, ln_g.reshape(1, SG_DIM), ln_b.reshape(1, SG_DIM),
      sg_w, sg_b.T)


def _softplus_terms(z):
    sp = jnp.maximum(z, 0.0) + jnp.log(1.0 + jnp.exp(-jnp.abs(z)))
    return sp, z - sp


def _split_bf16(x):
    hi = x.astype(BF16)
    lo = (x - hi.astype(F32)).astype(BF16)
    return hi, lo


def _sb_prompt_kernel(bias_ref, q_ref, k_ref, v_ref, o_ref, kb_ref, vb_ref, later_ref, *, tq, tk):
    b = pl.program_id(0)
    hp = pl.program_id(1)
    i = pl.program_id(2)
    blocks_per_tile = tq // tk

    @pl.when((b == 0) & (hp == 0) & (i == 0))
    def _():
        row = lax.broadcasted_iota(jnp.int32, (tk, tk), 0)
        col = lax.broadcasted_iota(jnp.int32, (tk, tk), 1)
        later = jnp.where(row > col, 1.0, 0.0).astype(BF16)
        later_ref[0:tk, :] = later
        later_ref[tk:2 * tk, :] = later

    @pl.when(i == 0)
    def _():
        kb_ref[...] = k_ref[0].astype(BF16)
        vb_ref[...] = v_ref[0].astype(BF16)

    scale = SB_HEAD_DIM ** -0.5
    q = q_ref[0].astype(BF16)

    heads = range(HEADS_PER_STEP)

    def block(j, carry, first_row):
        lanes = [slice(g * SB_HEAD_DIM, (g + 1) * SB_HEAD_DIM) for g in heads]
        start = pl.multiple_of(j * tk, tk)
        r0 = 0 if first_row is None else first_row
        acc = [carry[2 * g][r0:] for g in heads]
        run = [carry[2 * g + 1][r0:] for g in heads]
        z = [lax.dot_general(q[r0:, lanes[g]], kb_ref[pl.ds(start, tk), lanes[g]], NT_DIMS,
                             preferred_element_type=F32) * scale + bias_ref[hp * HEADS_PER_STEP + g]
             for g in heads]
        terms = [_softplus_terms(z[g]) for g in heads]
        sp = [terms[g][0] for g in heads]
        log_beta = [terms[g][1] for g in heads]
        if first_row is not None:
            row = lax.broadcasted_iota(jnp.int32, (tq - r0, tk), 0)
            col = lax.broadcasted_iota(jnp.int32, (tq - r0, tk), 1)
            visible = col < row
            sp = [jnp.where(visible, sp[g], 0.0) for g in heads]
        split = [_split_bf16(sp[g]) for g in heads]
        behind = [jnp.dot(jnp.concatenate(split[g], axis=1), later_ref[...], preferred_element_type=F32)
                  for g in heads]
        p = [jnp.exp(log_beta[g] - behind[g] - run[g]) for g in heads]
        if first_row is not None:
            p = [jnp.where(visible, p[g], 0.0) for g in heads]
        out = []
        for g in heads:
            vs = vb_ref[pl.ds(start, tk), lanes[g]]
            acc_g = acc[g] + jnp.dot(p[g].astype(BF16), vs, preferred_element_type=F32)
            run_g = run[g] + (behind[g][:, 0:1] + sp[g][:, 0:1])
            if r0:
                acc_g = jnp.concatenate([carry[2 * g][:r0], acc_g], axis=0)
                run_g = jnp.concatenate([carry[2 * g + 1][:r0], run_g], axis=0)
            out.extend((acc_g, run_g))
        return tuple(out)

    carry = tuple(c for _ in heads for c in (jnp.zeros((tq, SB_HEAD_DIM), F32), jnp.zeros((tq, 1), F32)))
    for d in reversed(range(blocks_per_tile)):
        carry = block(i * blocks_per_tile + d, carry, d * tk)
    carry = lax.fori_loop(0, i * blocks_per_tile,
                          lambda step, c: block(i * blocks_per_tile - 1 - step, c, None), carry)
    for g in heads:
        o_ref[0, :, g * SB_HEAD_DIM:(g + 1) * SB_HEAD_DIM] = carry[2 * g].astype(o_ref.dtype)


def sb_attn_prompt(q, kbuf, vbuf, layer, bias, *, tq, tk):
    bn, seq, _ = q.shape
    nq = seq // tq
    width = HEADS_PER_STEP * SB_HEAD_DIM
    kv_spec = pl.BlockSpec((None, 1, seq, width), lambda b, hp, i: (layer, b, 0, hp))
    return pl.pallas_call(
        functools.partial(_sb_prompt_kernel, tq=tq, tk=tk),
        out_shape=jax.ShapeDtypeStruct((bn, seq, D_MODEL), BF16),
        grid=(bn, SB_HEADS // HEADS_PER_STEP, nq),
        in_specs=[pl.BlockSpec(memory_space=pltpu.SMEM),
                  pl.BlockSpec((1, tq, width), lambda b, hp, i: (b, i, hp)),
                  kv_spec, kv_spec],
        out_specs=pl.BlockSpec((1, tq, width), lambda b, hp, i: (b, i, hp)),
        scratch_shapes=[pltpu.VMEM((seq, width), BF16), pltpu.VMEM((seq, width), BF16),
                        pltpu.VMEM((2 * tk, tk), BF16)],
        compiler_params=_cparams(3),
        name="sb_attn_prompt",
    )(bias, q, kbuf, vbuf)


def _head_of(pair_index):
    return pair_index & (SB_HEADS - 1)


def _key_of(pair_index):
    return pair_index >> (SB_HEADS.bit_length() - 1)


def _sb_sample_kernel(pt_ref, q_ref, bias_ref, knew_ref, vnew_ref, *rest, n_new, pages_per_step):
    kpg_refs = rest[0:pages_per_step]
    vpg_refs = rest[pages_per_step:2 * pages_per_step]
    o_ref, acc_ref, run_ref, new_ref, scanw_ref = rest[2 * pages_per_step:]
    b = pl.program_id(0)
    s = pl.program_id(1)
    n_steps = pl.num_programs(1)
    scale = SB_HEAD_DIM ** -0.5
    n_rows = n_new * SB_HEADS

    @pl.when((b == 0) & (s == 0))
    def _():
        r = lax.broadcasted_iota(jnp.int32, (V7X_LANES, 2 * V7X_LANES), 0)
        c = lax.broadcasted_iota(jnp.int32, (V7X_LANES, 2 * V7X_LANES), 1)
        same_head = _head_of(r) == _head_of(c)
        after = _key_of(r) > _key_of(c & (V7X_LANES - 1))
        w = jnp.where(same_head & (after | (c >= V7X_LANES)), 1.0, 0.0).astype(BF16)
        scanw_ref[0:V7X_LANES, :] = w
        scanw_ref[V7X_LANES:2 * V7X_LANES, :] = w

    head_row = lax.broadcasted_iota(jnp.int32, (SB_HEADS, V7X_LANES), 0)
    head_lane = _head_of(lax.broadcasted_iota(jnp.int32, (SB_HEADS, V7X_LANES), 1))
    own = head_row == head_lane
    query_row = lax.broadcasted_iota(jnp.int32, (V7X_SUBLANES, V7X_LANES), 0)

    def absorb(k2d, v2d, n_tiles, visible, acc, run):
        zfull = lax.dot_general(q_ref[0], k2d, NT_DIMS, preferred_element_type=F32)
        tiles = []
        for t in range(n_tiles):
            lanes = slice(t * V7X_LANES, (t + 1) * V7X_LANES)
            tile = jnp.zeros((V7X_SUBLANES, V7X_LANES), F32)
            for i in range(n_new):
                own_head = jnp.sum(jnp.where(own, zfull[i * SB_HEADS:(i + 1) * SB_HEADS, lanes], 0.0),
                                   axis=0, keepdims=True)
                tile = jnp.where(query_row == i, own_head, tile)
            tiles.append(tile)
        z = jnp.concatenate(tiles, axis=0) * scale + bias_ref[...]
        sp, log_beta = _softplus_terms(z)
        if visible is not None:
            sp = jnp.where(visible, sp, 0.0)
        hi, lo = _split_bf16(sp)
        scan = jnp.dot(jnp.concatenate([hi, lo], axis=1), scanw_ref[...], preferred_element_type=F32)
        p_tiles = [None] * n_tiles
        for t in reversed(range(n_tiles)):
            rs = slice(t * V7X_SUBLANES, (t + 1) * V7X_SUBLANES)
            p_t = jnp.exp(log_beta[rs] - scan[rs, 0:V7X_LANES] - run)
            if visible is not None:
                p_t = jnp.where(visible[rs], p_t, 0.0)
            p_tiles[t] = p_t
            run = run + scan[rs, V7X_LANES:2 * V7X_LANES]
        blocks = []
        for i in range(n_new):
            blocks.append(jnp.concatenate(
                [jnp.where(own, jnp.broadcast_to(p_tiles[t][i:i + 1, :], (SB_HEADS, V7X_LANES)), 0.0)
                 for t in range(n_tiles)], axis=1).astype(BF16))
        pbd = jnp.concatenate(blocks, axis=0)
        return acc + jnp.dot(pbd, v2d, preferred_element_type=F32), run

    @pl.when(s == 0)
    def _():
        new_ref[...] = jnp.zeros_like(new_ref)
        new_ref[0, 0:n_new] = knew_ref[0]
        new_ref[1, 0:n_new] = vnew_ref[0]
        key = _key_of(lax.broadcasted_iota(jnp.int32, (V7X_SUBLANES, V7X_LANES), 1))
        acc, run = absorb(new_ref[0].reshape(V7X_LANES, SB_HEAD_DIM).astype(BF16),
                          new_ref[1].reshape(V7X_LANES, SB_HEAD_DIM).astype(BF16), 1, key < query_row,
                          jnp.zeros(acc_ref.shape, F32), jnp.zeros(run_ref.shape, F32))
        acc_ref[...] = acc
        run_ref[...] = run

    @pl.when(s > 0)
    def _():
        n_rows_page = PAGE_SIZE * SB_HEADS
        acc, run = acc_ref[...], run_ref[...]
        for u in reversed(range(pages_per_step)):
            acc, run = absorb(kpg_refs[u][...].reshape(n_rows_page, SB_HEAD_DIM).astype(BF16),
                              vpg_refs[u][...].reshape(n_rows_page, SB_HEAD_DIM).astype(BF16),
                              n_rows_page // V7X_LANES, None, acc, run)
        acc_ref[...] = acc
        run_ref[...] = run

    @pl.when(s == n_steps - 1)
    def _():
        for i in range(n_new):
            for h in range(SB_HEADS):
                r = i * SB_HEADS + h
                o_ref[0, i:i + 1, h * SB_HEAD_DIM:(h + 1) * SB_HEAD_DIM] = acc_ref[r:r + 1, :]


def sb_attn_sample(q, kbuf, vbuf, layer, cache_k, cache_v, page_table, bias):
    bn, n_new, _ = q.shape
    assert n_new <= V7X_SUBLANES and KEYS_PER_TILE >= n_new
    n_layers = kbuf.shape[0]
    n_pages = page_table.shape[1]
    n_rows = n_new * SB_HEADS
    q_rows = q.reshape(bn, n_rows, SB_HEAD_DIM).astype(BF16)
    bias_l = jnp.tile(bias, KEYS_PER_TILE).reshape(1, V7X_LANES)
    k_new = kbuf.reshape(n_layers, bn, n_new, SB_HEADS, SB_HEAD_DIM)
    v_new = vbuf.reshape(n_layers, bn, n_new, SB_HEADS, SB_HEAD_DIM)

    pps = next(c for c in (SAMPLE_PAGES_PER_STEP, 2, 1) if n_pages % c == 0)

    def page_spec(u):
        def index(b, s, pt):
            return (layer, pt[b, n_pages - jnp.maximum(s, 1) * pps + u], 0, 0, 0)
        return pl.BlockSpec((None, None, PAGE_SIZE, SB_HEADS, SB_HEAD_DIM), index)

    new_spec = pl.BlockSpec((None, 1, n_new, SB_HEADS, SB_HEAD_DIM), lambda b, s, pt: (layer, b, 0, 0, 0))
    page_specs = [page_spec(u) for u in range(pps)]
    grid_spec = pltpu.PrefetchScalarGridSpec(
        num_scalar_prefetch=1,
        grid=(bn, n_pages // pps + 1),
        in_specs=[pl.BlockSpec((1, n_rows, SB_HEAD_DIM), lambda b, s, pt: (b, 0, 0)),
                  pl.BlockSpec((1, V7X_LANES), lambda b, s, pt: (0, 0)),
                  new_spec, new_spec] + page_specs + page_specs,
        out_specs=pl.BlockSpec((1, n_new, D_MODEL), lambda b, s, pt: (b, 0, 0)),
        scratch_shapes=[pltpu.VMEM((n_rows, SB_HEAD_DIM), F32),
                        pltpu.VMEM((V7X_SUBLANES, V7X_LANES), F32),
                        pltpu.VMEM((2, KEYS_PER_TILE, SB_HEADS, SB_HEAD_DIM), F32),
                        pltpu.VMEM((2 * V7X_LANES, 2 * V7X_LANES), BF16)],
    )
    return pl.pallas_call(
        functools.partial(_sb_sample_kernel, n_new=n_new, pages_per_step=pps),
        out_shape=jax.ShapeDtypeStruct((bn, n_new, D_MODEL), F32),
        grid_spec=grid_spec,
        compiler_params=_cparams(2),
        name="sb_attn_sample",
    )(page_table, q_rows, bias_l, k_new, v_new, *([cache_k] * pps), *([cache_v] * pps))


def _trunks(x3s, conv_states, attends, p, *, bm, even_rows):
    groups = range(len(x3s))
    shapes = [x3.shape for x3 in x3s]
    d = D_MODEL
    rows = [bn * seq for bn, seq, _ in shapes]
    n_odd = p['w_qkv'].shape[0]
    xs = [x3.reshape(rows[g], d) for g, x3 in enumerate(x3s)]
    hs = [rms_cast(x, p['norm_mix_pre'][0]) for x in xs]
    conv_new = [[] for _ in groups]
    sgv_new = [[] for _ in groups]
    kbufs = tuple(jnp.zeros((n_odd, rows[g], d), F32) for g in groups)
    vbufs = tuple(jnp.zeros((n_odd, rows[g], d), F32) for g in groups)
    for l in range(DEPTH):
        i = l // 2
        if l % 2 == 0:
            proj_dtypes = tuple(BF16 if seq % 16 == 0 else F32 for _, seq, _ in shapes)
            projs = matmul(hs, p['w_in_ab'], i, bm=bm, bn=1024, out_dtypes=proj_dtypes)
            mixed = []
            for g in groups:
                bn, seq, _ = shapes[g]
                y, c, vr = even_mix(projs[g].reshape(bn, seq, IN_AB), conv_states[g][i], p['conv_w'][i],
                                    p['sg_ln_g'][i], p['sg_ln_b'][i], p['sg_w'][i], p['sg_b'][i],
                                    rows=even_rows)
                conv_new[g].append(c)
                sgv_new[g].append(vr)
                mixed.append(y.reshape(rows[g], d))
            w_mix = p['w_out_ab']
        else:
            qs = matmul(hs, p['w_qkv'], i, bm=bm, bn=1024, col0=0, n_cols=d, out_dtypes=(BF16, BF16))
            kbufs = matmul(hs, p['w_qkv'], i, bm=bm, bn=1024, col0=d, n_cols=d, into=(kbufs, i))
            vbufs = matmul(hs, p['w_qkv'], i, bm=bm, bn=1024, col0=2 * d, n_cols=d, into=(vbufs, i))
            mixed = []
            for g in groups:
                bn, seq, _ = shapes[g]
                o = attends[g](i, qs[g].reshape(bn, seq, d), kbufs[g].reshape(n_odd, bn, seq, d),
                               vbufs[g].reshape(n_odd, bn, seq, d))
                mixed.append(o.reshape(rows[g], d))
            w_mix = p['w_o']
        xs, hs = matmul_post_norm(mixed, w_mix, i, xs, p['norm_mix_post'][l], p['norm_ffn_pre'][l], bm=512)
        a = swiglu_up(hs, p['w_gate'], p['w_up'], l, bm=bm, bn=512)
        ms = matmul(a, p['w_down'], l, bm=512, bn=512, out_dtypes=(BF16, BF16))
        g_next = p['norm_mix_pre'][l + 1] if l + 1 < DEPTH else None
        xs, hs = zip(*[post_norm(xs[g], ms[g], p['norm_ffn_post'][l], g_next) for g in groups])
    out = []
    for g in groups:
        bn, seq, _ = shapes[g]
        kv_shape = (n_odd, bn, seq, SB_HEADS, SB_HEAD_DIM)
        out.append((xs[g].reshape(bn, seq, d), jnp.stack(conv_new[g]), jnp.stack(sgv_new[g]),
                    kbufs[g].reshape(kv_shape), vbufs[g].reshape(kv_shape)))
    return out


def kernel(x_prompt, x_sample, state_conv, cache_k, cache_v, page_table, norm_mix_pre, norm_mix_post,
           norm_ffn_pre, norm_ffn_post, w_in_ab, conv_w, sg_ln_g, sg_ln_b, sg_w, sg_b, w_out_ab, w_qkv,
           sb_bias, w_o, w_gate, w_up, w_down):
    p = dict(norm_mix_pre=norm_mix_pre, norm_mix_post=norm_mix_post, norm_ffn_pre=norm_ffn_pre,
             norm_ffn_post=norm_ffn_post, w_in_ab=w_in_ab, conv_w=conv_w, sg_ln_g=sg_ln_g,
             sg_ln_b=sg_ln_b, sg_w=sg_w, sg_b=sg_b, w_out_ab=w_out_ab, w_qkv=w_qkv, sb_bias=sb_bias,
             w_o=w_o, w_gate=w_gate, w_up=w_up, w_down=w_down)
    n_even = state_conv.shape[0]

    bp = x_prompt.shape[0]
    conv0 = jnp.zeros((n_even, bp, CONV_W - 1, CONV_DIM), x_prompt.dtype)

    def attend_prompt(i, q, kbuf, vbuf):
        return sb_attn_prompt(q, kbuf, vbuf, i, sb_bias[i], tq=512, tk=256)

    def attend_sample(i, q, kbuf, vbuf):
        return sb_attn_sample(q, kbuf, vbuf, i, cache_k, cache_v, page_table, sb_bias[i])

    (y_prompt, conv_p, sgv_p, k_p, v_p), (y_sample, conv_s, sgv_s, k_s, v_s) = _trunks(
        (x_prompt, x_sample), (conv0, state_conv), (attend_prompt, attend_sample), p, bm=1024, even_rows=256)
    return (y_prompt, y_sample, conv_p, conv_s, sgv_p, sgv_s, k_p, v_p, k_s, v_s)
```

```python
import functools

import jax
import jax.numpy as jnp
from jax import lax
from jax.experimental import pallas as pl
from jax.experimental.pallas import tpu as pltpu

D_MODEL = 2048
DEPTH = 4
PAGE_SIZE = 128
CONV_DIM = D_MODEL // 2
CONV_W = 3
SG_DIM = D_MODEL // 2
SG_HEADS = 8
SG_HEAD_DIM = SG_DIM // SG_HEADS
CHUNK = 128
SB_HEADS = 16
SB_HEAD_DIM = D_MODEL // SB_HEADS
D_FF = 5632
IN_AB = 3 * CONV_DIM + 2 * SG_DIM
EPS = 1e-6

V7X_LANES = 128
V7X_SUBLANES = 8
V7X_VMEM_LIMIT_BYTES = 56 * 1024 * 1024

BF16 = jnp.bfloat16
F32 = jnp.float32

HEADS_PER_STEP = 4
SAMPLE_PAGES_PER_STEP = 8
KEYS_PER_TILE = V7X_LANES // SB_HEADS
NT_DIMS = (((1,), (1,)), ((), ()))


def _cparams(n_axes):
    return pltpu.CompilerParams(
        dimension_semantics=("arbitrary",) * n_axes,
        vmem_limit_bytes=V7X_VMEM_LIMIT_BYTES,
    )


def _row_block(m, want):
    return want if m % want == 0 else m


def _rms_cast_kernel(x_ref, g_ref, h_ref):
    x = x_ref[...]
    y = x * lax.rsqrt(jnp.mean(x * x, axis=-1, keepdims=True) + EPS)
    h_ref[...] = (y * g_ref[...]).astype(h_ref.dtype)


def rms_cast(x, g):
    m, d = x.shape
    bm = _row_block(m, 512)
    return pl.pallas_call(
        _rms_cast_kernel,
        out_shape=jax.ShapeDtypeStruct((m, d), BF16),
        grid=(m // bm,),
        in_specs=[pl.BlockSpec((bm, d), lambda i: (i, 0)),
                  pl.BlockSpec((1, d), lambda i: (0, 0))],
        out_specs=pl.BlockSpec((bm, d), lambda i: (i, 0)),
        compiler_params=_cparams(1),
        name="rms_cast",
    )(x, g.reshape(1, d))


def _post_norm_kernel(x_ref, m_ref, gp_ref, gn_ref, xo_ref, *maybe_h_ref):
    m = m_ref[...].astype(F32)
    y = m * lax.rsqrt(jnp.mean(m * m, axis=-1, keepdims=True) + EPS)
    xn = x_ref[...] + y * gp_ref[...]
    xo_ref[...] = xn
    for h_ref in maybe_h_ref:
        hn = xn * lax.rsqrt(jnp.mean(xn * xn, axis=-1, keepdims=True) + EPS)
        h_ref[...] = (hn * gn_ref[...]).astype(h_ref.dtype)


def post_norm(x, m, g_post, g_next):
    rows, d = x.shape
    bm = _row_block(rows, 512)
    with_next = g_next is not None
    row_spec = pl.BlockSpec((bm, d), lambda i: (i, 0))
    gain_spec = pl.BlockSpec((1, d), lambda i: (0, 0))
    out_shape = [jax.ShapeDtypeStruct((rows, d), F32)]
    if with_next:
        out_shape.append(jax.ShapeDtypeStruct((rows, d), BF16))
    else:
        g_next = g_post
    out = pl.pallas_call(
        _post_norm_kernel,
        out_shape=tuple(out_shape),
        grid=(rows // bm,),
        in_specs=[row_spec, row_spec, gain_spec, gain_spec],
        out_specs=tuple([row_spec] * len(out_shape)),
        compiler_params=_cparams(1),
        name="post_norm",
    )(x, m, g_post.reshape(1, d), g_next.reshape(1, d))
    return (out[0], out[1]) if with_next else (out[0], None)


def _on_group(row_step, fn, main_refs, small_refs):
    pl.when(row_step == 0)(lambda: fn(*small_refs))
    pl.when(row_step > 0)(lambda: fn(*main_refs))


def _main_row(row_step):
    return jnp.maximum(row_step - 1, 0)


def _mm_kernel(x1_ref, x2_ref, w_ref, *rest):
    o1_ref, o2_ref, wb_ref = rest[-3:]
    i = pl.program_id(1)

    @pl.when(i == 0)
    def _():
        wb_ref[...] = w_ref[...].astype(BF16)

    def product(x_ref, o_ref):
        o_ref[...] = jnp.dot(x_ref[...].astype(BF16), wb_ref[...],
                             preferred_element_type=F32).astype(o_ref.dtype)

    _on_group(i, product, (x1_ref, o1_ref), (x2_ref, o2_ref))


def matmul(xs, w, layer, *, bm, bn, out_dtypes=(F32, F32), col0=0, n_cols=None, into=None):
    (m1, k), (m2, _) = xs[0].shape, xs[1].shape
    n_cols = w.shape[2] if n_cols is None else n_cols
    n_main = m1 // bm
    jb0 = col0 // bn
    main_row = _main_row
    in_specs = [pl.BlockSpec((bm, k), lambda j, i: (main_row(i), 0)),
                pl.BlockSpec((m2, k), lambda j, i: (0, 0)),
                pl.BlockSpec((None, k, bn), lambda j, i: (layer, 0, jb0 + j))]
    args = [xs[0], xs[1], w]
    if into is None:
        out_shape = (jax.ShapeDtypeStruct((m1, n_cols), out_dtypes[0]),
                     jax.ShapeDtypeStruct((m2, n_cols), out_dtypes[1]))
        out_specs = (pl.BlockSpec((bm, bn), lambda j, i: (main_row(i), j)),
                     pl.BlockSpec((m2, bn), lambda j, i: (0, j)))
        aliases = {}
    else:
        bufs, slot = into
        out_shape = tuple(jax.ShapeDtypeStruct(b.shape, b.dtype) for b in bufs)
        out_specs = (pl.BlockSpec((None, bm, bn), lambda j, i: (slot, main_row(i), j)),
                     pl.BlockSpec((None, m2, bn), lambda j, i: (slot, 0, j)))
        in_specs += [pl.BlockSpec(memory_space=pl.ANY)] * 2
        args += list(bufs)
        aliases = {3: 0, 4: 1}
    return pl.pallas_call(
        _mm_kernel,
        out_shape=out_shape,
        grid=(n_cols // bn, n_main + 1),
        in_specs=in_specs,
        out_specs=out_specs,
        scratch_shapes=[pltpu.VMEM((k, bn), BF16)],
        input_output_aliases=aliases,
        compiler_params=_cparams(2),
        name="matmul",
    )(*args)


def _mm_post_norm_kernel(y1_ref, y2_ref, w_ref, x1_ref, x2_ref, gp_ref, gn_ref,
                         xo1_ref, h1_ref, xo2_ref, h2_ref, wb_ref):
    i = pl.program_id(0)

    @pl.when(i == 0)
    def _():
        wb_ref[...] = w_ref[...].astype(BF16)

    def project(y_ref, x_ref, xo_ref, h_ref):
        m = jnp.dot(y_ref[...].astype(BF16), wb_ref[...], preferred_element_type=F32)
        nm = m * lax.rsqrt(jnp.mean(m * m, axis=-1, keepdims=True) + EPS)
        xn = x_ref[...] + nm * gp_ref[...]
        xo_ref[...] = xn
        hn = xn * lax.rsqrt(jnp.mean(xn * xn, axis=-1, keepdims=True) + EPS)
        h_ref[...] = (hn * gn_ref[...]).astype(h_ref.dtype)

    _on_group(i, project, (y1_ref, x1_ref, xo1_ref, h1_ref), (y2_ref, x2_ref, xo2_ref, h2_ref))


def matmul_post_norm(ys, w, layer, xs, g_post, g_next, *, bm):
    (m1, k), (m2, _) = ys[0].shape, ys[1].shape
    n = w.shape[2]
    n_main = m1 // bm
    main_in = pl.BlockSpec((bm, k), lambda i: (_main_row(i), 0))
    main_row_spec = pl.BlockSpec((bm, n), lambda i: (_main_row(i), 0))
    tail_in = pl.BlockSpec((m2, k), lambda i: (0, 0))
    tail_row_spec = pl.BlockSpec((m2, n), lambda i: (0, 0))
    gain_spec = pl.BlockSpec((1, n), lambda i: (0, 0))
    xo1, h1, xo2, h2 = pl.pallas_call(
        _mm_post_norm_kernel,
        out_shape=(jax.ShapeDtypeStruct((m1, n), F32), jax.ShapeDtypeStruct((m1, n), BF16),
                   jax.ShapeDtypeStruct((m2, n), F32), jax.ShapeDtypeStruct((m2, n), BF16)),
        grid=(n_main + 1,),
        in_specs=[main_in, tail_in,
                  pl.BlockSpec((None, k, n), lambda i: (layer, 0, 0), pipeline_mode=pl.Buffered(1)),
                  main_row_spec, tail_row_spec, gain_spec, gain_spec],
        out_specs=(main_row_spec, main_row_spec, tail_row_spec, tail_row_spec),
        scratch_shapes=[pltpu.VMEM((k, n), BF16)],
        compiler_params=_cparams(1),
        name="matmul_post_norm",
    )(ys[0], ys[1], w, xs[0], xs[1], g_post.reshape(1, n), g_next.reshape(1, n))
    return (xo1, xo2), (h1, h2)


def _swiglu_up_kernel(x1_ref, x2_ref, wg_ref, wu_ref, o1_ref, o2_ref, wgb_ref, wub_ref):
    i = pl.program_id(1)

    @pl.when(i == 0)
    def _():
        wgb_ref[...] = wg_ref[...].astype(BF16)
        wub_ref[...] = wu_ref[...].astype(BF16)

    def gated(x_ref, o_ref):
        x = x_ref[...]
        g = jnp.dot(x, wgb_ref[...], preferred_element_type=F32)
        u = jnp.dot(x, wub_ref[...], preferred_element_type=F32)
        o_ref[...] = (g * jax.nn.sigmoid(g) * u).astype(o_ref.dtype)

    _on_group(i, gated, (x1_ref, o1_ref), (x2_ref, o2_ref))


def swiglu_up(xs, wg, wu, layer, *, bm, bn):
    (m1, k), (m2, _) = xs[0].shape, xs[1].shape
    n = wg.shape[2]
    n_main = m1 // bm
    main_row = _main_row
    w_spec = pl.BlockSpec((None, k, bn), lambda j, i: (layer, 0, j))
    return pl.pallas_call(
        _swiglu_up_kernel,
        out_shape=(jax.ShapeDtypeStruct((m1, n), BF16), jax.ShapeDtypeStruct((m2, n), BF16)),
        grid=(n // bn, n_main + 1),
        in_specs=[pl.BlockSpec((bm, k), lambda j, i: (main_row(i), 0)),
                  pl.BlockSpec((m2, k), lambda j, i: (0, 0)),
                  w_spec, w_spec],
        out_specs=(pl.BlockSpec((bm, bn), lambda j, i: (main_row(i), j)),
                   pl.BlockSpec((m2, bn), lambda j, i: (0, j))),
        scratch_shapes=[pltpu.VMEM((k, bn), BF16), pltpu.VMEM((k, bn), BF16)],
        compiler_params=_cparams(2),
        name="swiglu_up",
    )(xs[0], xs[1], wg, wu)


def _even_mix_kernel(gb_ref, gc_ref, xa_ref, u_ref, vv_ref, prefix_ref, cw_ref, lng_ref, lnb_ref,
                     sgw_ref, sgbt_ref, y_ref, cnew_ref, vlast_ref, cbuf_ref, wtri_ref, pad_ref,
                     *, rows, n_steps):
    b = pl.program_id(0)
    l = pl.program_id(1)
    carry_at = V7X_SUBLANES - (CONV_W - 1)

    @pl.when((b == 0) & (l == 0))
    def _():
        t = lax.broadcasted_iota(jnp.int32, (CHUNK, CHUNK), 0)
        s = lax.broadcasted_iota(jnp.int32, (CHUNK, CHUNK), 1)
        for h in range(SG_HEADS):
            wtri_ref[h] = jnp.where(s <= t, sgw_ref[h], 0.0).astype(BF16)

    @pl.when(l == 0)
    def _():
        cbuf_ref[carry_at:V7X_SUBLANES, :] = prefix_ref[0]

    c = gc_ref[0].astype(F32) * xa_ref[0].astype(F32)
    cbuf_ref[V7X_SUBLANES:V7X_SUBLANES + rows, :] = c
    conv = cw_ref[CONV_W - 1:CONV_W, :] * c
    for j in range(CONV_W - 1):
        conv = conv + cw_ref[j:j + 1, :] * cbuf_ref[carry_at + j:carry_at + j + rows, :]
    y_ref[0, :, 0:CONV_DIM] = (gb_ref[0].astype(F32) * conv).astype(y_ref.dtype)
    new_hist = cbuf_ref[V7X_SUBLANES + rows - (CONV_W - 1):V7X_SUBLANES + rows, :]
    cbuf_ref[carry_at:V7X_SUBLANES, :] = new_hist

    @pl.when(l == n_steps - 1)
    def _():
        cnew_ref[0] = new_hist

    vv = vv_ref[0].astype(F32)
    mu = jnp.mean(vv, axis=-1, keepdims=True)
    xc = vv - mu
    vn = xc * lax.rsqrt(jnp.mean(xc * xc, axis=-1, keepdims=True) + EPS) * lng_ref[...] + lnb_ref[...]

    n_chunks = -(-rows // CHUNK)
    last_rows = rows - (n_chunks - 1) * CHUNK

    @pl.when(l == n_steps - 1)
    def _():
        vlast_ref[0] = vn[(n_chunks - 1) * CHUNK:, :]

    if rows % CHUNK != 0:
        pad_ref[...] = jnp.zeros_like(pad_ref)
        pad_ref[0:rows, :] = vn
        vn_full = pad_ref[...]
    else:
        vn_full = vn
    vnb = vn_full.astype(BF16)
    u = u_ref[0].astype(F32)
    for ci in range(n_chunks):
        r0 = ci * CHUNK
        nr = CHUNK if ci < n_chunks - 1 else last_rows
        for h in range(SG_HEADS):
            c0 = h * SG_HEAD_DIM
            s = jnp.dot(wtri_ref[h], vnb[r0:r0 + CHUNK, c0:c0 + SG_HEAD_DIM], preferred_element_type=F32)
            s = s + sgbt_ref[:, h:h + 1]
            yb = u[r0:r0 + nr, c0:c0 + SG_HEAD_DIM] * s[0:nr]
            y_ref[0, r0:r0 + nr, CONV_DIM + c0:CONV_DIM + c0 + SG_HEAD_DIM] = yb.astype(y_ref.dtype)


def even_mix(proj, prefix, conv_w, ln_g, ln_b, sg_w, sg_b, *, rows):
    bn, seq, _ = proj.shape
    rows = min(rows, seq)
    n_steps = seq // rows
    r_last = seq - ((seq - 1) // CHUNK) * CHUNK
    pad_rows = -(-rows // CHUNK) * CHUNK

    def col(cb):
        return pl.BlockSpec((1, rows, CONV_DIM), lambda b, l: (b, l, cb))

    const2 = lambda b, l: (0, 0)
    y_dtype = BF16 if rows % 16 == 0 else F32
    return pl.pallas_call(
        functools.partial(_even_mix_kernel, rows=rows, n_steps=n_steps),
        out_shape=(jax.ShapeDtypeStruct((bn, seq, D_MODEL), y_dtype),
                   jax.ShapeDtypeStruct((bn, CONV_W - 1, CONV_DIM), F32),
                   jax.ShapeDtypeStruct((bn, r_last, SG_DIM), F32)),
        grid=(bn, n_steps),
        in_specs=[col(0), col(1), col(2), col(3), col(4),
                  pl.BlockSpec((1, CONV_W - 1, CONV_DIM), lambda b, l: (b, 0, 0)),
                  pl.BlockSpec((CONV_W, CONV_DIM), const2),
                  pl.BlockSpec((1, SG_DIM), const2),
                  pl.BlockSpec((1, SG_DIM), const2),
                  pl.BlockSpec((SG_HEADS, CHUNK, CHUNK), lambda b, l: (0, 0, 0)),
                  pl.BlockSpec((CHUNK, SG_HEADS), const2)],
        out_specs=(pl.BlockSpec((1, rows, D_MODEL), lambda b, l: (b, l, 0)),
                   pl.BlockSpec((1, CONV_W - 1, CONV_DIM), lambda b, l: (b, 0, 0)),
                   pl.BlockSpec((1, r_last, SG_DIM), lambda b, l: (b, 0, 0))),
        scratch_shapes=[pltpu.VMEM((V7X_SUBLANES + rows, CONV_DIM), F32),
                        pltpu.VMEM((SG_HEADS, CHUNK, CHUNK), BF16),
                        pltpu.VMEM((pad_rows, SG_DIM), F32)],
        compiler_params=_cparams(2),
        name="even_mix",
    )(proj, proj, proj, proj, proj, prefix, conv_w, ln_g.reshape(1, SG_DIM), ln_b.reshape(1, SG_DIM),
      sg_w, sg_b.T)


def _softplus_terms(z):
    sp = jnp.maximum(z, 0.0) + jnp.log(1.0 + jnp.exp(-jnp.abs(z)))
    return sp, z - sp


def _split_bf16(x):
    hi = x.astype(BF16)
    lo = (x - hi.astype(F32)).astype(BF16)
    return hi, lo


def _sb_prompt_kernel(bias_ref, q_ref, k_ref, v_ref, o_ref, kb_ref, vb_ref, later_ref, *, tq, tk):
    b = pl.program_id(0)
    hp = pl.program_id(1)
    i = pl.program_id(2)
    blocks_per_tile = tq // tk

    @pl.when((b == 0) & (hp == 0) & (i == 0))
    def _():
        row = lax.broadcasted_iota(jnp.int32, (tk, tk), 0)
        col = lax.broadcasted_iota(jnp.int32, (tk, tk), 1)
        later = jnp.where(row > col, 1.0, 0.0).astype(BF16)
        later_ref[0:tk, :] = later
        later_ref[tk:2 * tk, :] = later

    @pl.when(i == 0)
    def _():
        kb_ref[...] = k_ref[0].astype(BF16)
        vb_ref[...] = v_ref[0].astype(BF16)

    scale = SB_HEAD_DIM ** -0.5
    q = q_ref[0].astype(BF16)

    heads = range(HEADS_PER_STEP)

    def block(j, carry, first_row):
        lanes = [slice(g * SB_HEAD_DIM, (g + 1) * SB_HEAD_DIM) for g in heads]
        start = pl.multiple_of(j * tk, tk)
        r0 = 0 if first_row is None else first_row
        acc = [carry[2 * g][r0:] for g in heads]
        run = [carry[2 * g + 1][r0:] for g in heads]
        z = [lax.dot_general(q[r0:, lanes[g]], kb_ref[pl.ds(start, tk), lanes[g]], NT_DIMS,
                             preferred_element_type=F32) * scale + bias_ref[hp * HEADS_PER_STEP + g]
             for g in heads]
        terms = [_softplus_terms(z[g]) for g in heads]
        sp = [terms[g][0] for g in heads]
        log_beta = [terms[g][1] for g in heads]
        if first_row is not None:
            row = lax.broadcasted_iota(jnp.int32, (tq - r0, tk), 0)
            col = lax.broadcasted_iota(jnp.int32, (tq - r0, tk), 1)
            visible = col < row
            sp = [jnp.where(visible, sp[g], 0.0) for g in heads]
        split = [_split_bf16(sp[g]) for g in heads]
        behind = [jnp.dot(jnp.concatenate(split[g], axis=1), later_ref[...], preferred_element_type=F32)
                  for g in heads]
        p = [jnp.exp(log_beta[g] - behind[g] - run[g]) for g in heads]
        if first_row is not None:
            p = [jnp.where(visible, p[g], 0.0) for g in heads]
        out = []
        for g in heads:
            vs = vb_ref[pl.ds(start, tk), lanes[g]]
            acc_g = acc[g] + jnp.dot(p[g].astype(BF16), vs, preferred_element_type=F32)
            run_g = run[g] + (behind[g][:, 0:1] + sp[g][:, 0:1])
            if r0:
                acc_g = jnp.concatenate([carry[2 * g][:r0], acc_g], axis=0)
                run_g = jnp.concatenate([carry[2 * g + 1][:r0], run_g], axis=0)
            out.extend((acc_g, run_g))
        return tuple(out)

    carry = tuple(c for _ in heads for c in (jnp.zeros((tq, SB_HEAD_DIM), F32), jnp.zeros((tq, 1), F32)))
    for d in reversed(range(blocks_per_tile)):
        carry = block(i * blocks_per_tile + d, carry, d * tk)
    carry = lax.fori_loop(0, i * blocks_per_tile,
                          lambda step, c: block(i * blocks_per_tile - 1 - step, c, None), carry)
    for g in heads:
        o_ref[0, :, g * SB_HEAD_DIM:(g + 1) * SB_HEAD_DIM] = carry[2 * g].astype(o_ref.dtype)


def sb_attn_prompt(q, kbuf, vbuf, layer, bias, *, tq, tk):
    bn, seq, _ = q.shape
    nq = seq // tq
    width = HEADS_PER_STEP * SB_HEAD_DIM
    kv_spec = pl.BlockSpec((None, 1, seq, width), lambda b, hp, i: (layer, b, 0, hp))
    return pl.pallas_call(
        functools.partial(_sb_prompt_kernel, tq=tq, tk=tk),
        out_shape=jax.ShapeDtypeStruct((bn, seq, D_MODEL), BF16),
        grid=(bn, SB_HEADS // HEADS_PER_STEP, nq),
        in_specs=[pl.BlockSpec(memory_space=pltpu.SMEM),
                  pl.BlockSpec((1, tq, width), lambda b, hp, i: (b, i, hp)),
                  kv_spec, kv_spec],
        out_specs=pl.BlockSpec((1, tq, width), lambda b, hp, i: (b, i, hp)),
        scratch_shapes=[pltpu.VMEM((seq, width), BF16), pltpu.VMEM((seq, width), BF16),
                        pltpu.VMEM((2 * tk, tk), BF16)],
        compiler_params=_cparams(3),
        name="sb_attn_prompt",
    )(bias, q, kbuf, vbuf)


def _head_of(pair_index):
    return pair_index & (SB_HEADS - 1)


def _key_of(pair_index):
    return pair_index >> (SB_HEADS.bit_length() - 1)


def _sb_sample_kernel(pt_ref, q_ref, bias_ref, knew_ref, vnew_ref, *rest, n_new, pages_per_step):
    kpg_refs = rest[0:pages_per_step]
    vpg_refs = rest[pages_per_step:2 * pages_per_step]
    o_ref, acc_ref, run_ref, new_ref, scanw_ref = rest[2 * pages_per_step:]
    b = pl.program_id(0)
    s = pl.program_id(1)
    n_steps = pl.num_programs(1)
    scale = SB_HEAD_DIM ** -0.5
    n_rows = n_new * SB_HEADS

    @pl.when((b == 0) & (s == 0))
    def _():
        r = lax.broadcasted_iota(jnp.int32, (V7X_LANES, 2 * V7X_LANES), 0)
        c = lax.broadcasted_iota(jnp.int32, (V7X_LANES, 2 * V7X_LANES), 1)
        same_head = _head_of(r) == _head_of(c)
        after = _key_of(r) > _key_of(c & (V7X_LANES - 1))
        w = jnp.where(same_head & (after | (c >= V7X_LANES)), 1.0, 0.0).astype(BF16)
        scanw_ref[0:V7X_LANES, :] = w
        scanw_ref[V7X_LANES:2 * V7X_LANES, :] = w

    head_row = lax.broadcasted_iota(jnp.int32, (SB_HEADS, V7X_LANES), 0)
    head_lane = _head_of(lax.broadcasted_iota(jnp.int32, (SB_HEADS, V7X_LANES), 1))
    own = head_row == head_lane
    query_row = lax.broadcasted_iota(jnp.int32, (V7X_SUBLANES, V7X_LANES), 0)

    def absorb(k2d, v2d, n_tiles, visible, acc, run):
        zfull = lax.dot_general(q_ref[0], k2d, NT_DIMS, preferred_element_type=F32)
        tiles = []
        for t in range(n_tiles):
            lanes = slice(t * V7X_LANES, (t + 1) * V7X_LANES)
            tile = jnp.zeros((V7X_SUBLANES, V7X_LANES), F32)
            for i in range(n_new):
                own_head = jnp.sum(jnp.where(own, zfull[i * SB_HEADS:(i + 1) * SB_HEADS, lanes], 0.0),
                                   axis=0, keepdims=True)
                tile = jnp.where(query_row == i, own_head, tile)
            tiles.append(tile)
        z = jnp.concatenate(tiles, axis=0) * scale + bias_ref[...]
        sp, log_beta = _softplus_terms(z)
        if visible is not None:
            sp = jnp.where(visible, sp, 0.0)
        hi, lo = _split_bf16(sp)
        scan = jnp.dot(jnp.concatenate([hi, lo], axis=1), scanw_ref[...], preferred_element_type=F32)
        p_tiles = [None] * n_tiles
        for t in reversed(range(n_tiles)):
            rs = slice(t * V7X_SUBLANES, (t + 1) * V7X_SUBLANES)
            p_t = jnp.exp(log_beta[rs] - scan[rs, 0:V7X_LANES] - run)
            if visible is not None:
                p_t = jnp.where(visible[rs], p_t, 0.0)
            p_tiles[t] = p_t
            run = run + scan[rs, V7X_LANES:2 * V7X_LANES]
        blocks = []
        for i in range(n_new):
            blocks.append(jnp.concatenate(
                [jnp.where(own, jnp.broadcast_to(p_tiles[t][i:i + 1, :], (SB_HEADS, V7X_LANES)), 0.0)
                 for t in range(n_tiles)], axis=1).astype(BF16))
        pbd = jnp.concatenate(blocks, axis=0)
        return acc + jnp.dot(pbd, v2d, preferred_element_type=F32), run

    @pl.when(s == 0)
    def _():
        new_ref[...] = jnp.zeros_like(new_ref)
        new_ref[0, 0:n_new] = knew_ref[0]
        new_ref[1, 0:n_new] = vnew_ref[0]
        key = _key_of(lax.broadcasted_iota(jnp.int32, (V7X_SUBLANES, V7X_LANES), 1))
        acc, run = absorb(new_ref[0].reshape(V7X_LANES, SB_HEAD_DIM).astype(BF16),
                          new_ref[1].reshape(V7X_LANES, SB_HEAD_DIM).astype(BF16), 1, key < query_row,
                          jnp.zeros(acc_ref.shape, F32), jnp.zeros(run_ref.shape, F32))
        acc_ref[...] = acc
        run_ref[...] = run

    @pl.when(s > 0)
    def _():
        n_rows_page = PAGE_SIZE * SB_HEADS
        acc, run = acc_ref[...], run_ref[...]
        for u in reversed(range(pages_per_step)):
            acc, run = absorb(kpg_refs[u][...].reshape(n_rows_page, SB_HEAD_DIM).astype(BF16),
                              vpg_refs[u][...].reshape(n_rows_page, SB_HEAD_DIM).astype(BF16),
                              n_rows_page // V7X_LANES, None, acc, run)
        acc_ref[...] = acc
        run_ref[...] = run

    @pl.when(s == n_steps - 1)
    def _():
        for i in range(n_new):
            for h in range(SB_HEADS):
                r = i * SB_HEADS + h
                o_ref[0, i:i + 1, h * SB_HEAD_DIM:(h + 1) * SB_HEAD_DIM] = acc_ref[r:r + 1, :]


def sb_attn_sample(q, kbuf, vbuf, layer, cache_k, cache_v, page_table, bias):
    bn, n_new, _ = q.shape
    assert n_new <= V7X_SUBLANES and KEYS_PER_TILE >= n_new
    n_layers = kbuf.shape[0]
    n_pages = page_table.shape[1]
    n_rows = n_new * SB_HEADS
    q_rows = q.reshape(bn, n_rows, SB_HEAD_DIM).astype(BF16)
    bias_l = jnp.tile(bias, KEYS_PER_TILE).reshape(1, V7X_LANES)
    k_new = kbuf.reshape(n_layers, bn, n_new, SB_HEADS, SB_HEAD_DIM)
    v_new = vbuf.reshape(n_layers, bn, n_new, SB_HEADS, SB_HEAD_DIM)

    pps = next(c for c in (SAMPLE_PAGES_PER_STEP, 2, 1) if n_pages % c == 0)

    def page_spec(u):
        def index(b, s, pt):
            return (layer, pt[b, n_pages - jnp.maximum(s, 1) * pps + u], 0, 0, 0)
        return pl.BlockSpec((None, None, PAGE_SIZE, SB_HEADS, SB_HEAD_DIM), index)

    new_spec = pl.BlockSpec((None, 1, n_new, SB_HEADS, SB_HEAD_DIM), lambda b, s, pt: (layer, b, 0, 0, 0))
    page_specs = [page_spec(u) for u in range(pps)]
    grid_spec = pltpu.PrefetchScalarGridSpec(
        num_scalar_prefetch=1,
        grid=(bn, n_pages // pps + 1),
        in_specs=[pl.BlockSpec((1, n_rows, SB_HEAD_DIM), lambda b, s, pt: (b, 0, 0)),
                  pl.BlockSpec((1, V7X_LANES), lambda b, s, pt: (0, 0)),
                  new_spec, new_spec] + page_specs + page_specs,
        out_specs=pl.BlockSpec((1, n_new, D_MODEL), lambda b, s, pt: (b, 0, 0)),
        scratch_shapes=[pltpu.VMEM((n_rows, SB_HEAD_DIM), F32),
                        pltpu.VMEM((V7X_SUBLANES, V7X_LANES), F32),
                        pltpu.VMEM((2, KEYS_PER_TILE, SB_HEADS, SB_HEAD_DIM), F32),
                        pltpu.VMEM((2 * V7X_LANES, 2 * V7X_LANES), BF16)],
    )
    return pl.pallas_call(
        functools.partial(_sb_sample_kernel, n_new=n_new, pages_per_step=pps),
        out_shape=jax.ShapeDtypeStruct((bn, n_new, D_MODEL), F32),
        grid_spec=grid_spec,
        compiler_params=_cparams(2),
        name="sb_attn_sample",
    )(page_table, q_rows, bias_l, k_new, v_new, *([cache_k] * pps), *([cache_v] * pps))


def _trunks(x3s, conv_states, attends, p, *, bm, even_rows):
    groups = range(len(x3s))
    shapes = [x3.shape for x3 in x3s]
    d = D_MODEL
    rows = [bn * seq for bn, seq, _ in shapes]
    n_odd = p['w_qkv'].shape[0]
    xs = [x3.reshape(rows[g], d) for g, x3 in enumerate(x3s)]
    hs = [rms_cast(x, p['norm_mix_pre'][0]) for x in xs]
    conv_new = [[] for _ in groups]
    sgv_new = [[] for _ in groups]
    kbufs = tuple(jnp.zeros((n_odd, rows[g], d), F32) for g in groups)
    vbufs = tuple(jnp.zeros((n_odd, rows[g], d), F32) for g in groups)
    for l in range(DEPTH):
        i = l // 2
        if l % 2 == 0:
            proj_dtypes = tuple(BF16 if seq % 16 == 0 else F32 for _, seq, _ in shapes)
            projs = matmul(hs, p['w_in_ab'], i, bm=bm, bn=1024, out_dtypes=proj_dtypes)
            mixed = []
            for g in groups:
                bn, seq, _ = shapes[g]
                y, c, vr = even_mix(projs[g].reshape(bn, seq, IN_AB), conv_states[g][i], p['conv_w'][i],
                                    p['sg_ln_g'][i], p['sg_ln_b'][i], p['sg_w'][i], p['sg_b'][i],
                                    rows=even_rows)
                conv_new[g].append(c)
                sgv_new[g].append(vr)
                mixed.append(y.reshape(rows[g], d))
            w_mix = p['w_out_ab']
        else:
            qs = matmul(hs, p['w_qkv'], i, bm=bm, bn=1024, col0=0, n_cols=d, out_dtypes=(BF16, BF16))
            kbufs = matmul(hs, p['w_qkv'], i, bm=bm, bn=1024, col0=d, n_cols=d, into=(kbufs, i))
            vbufs = matmul(hs, p['w_qkv'], i, bm=bm, bn=1024, col0=2 * d, n_cols=d, into=(vbufs, i))
            mixed = []
            for g in groups:
                bn, seq, _ = shapes[g]
                o = attends[g](i, qs[g].reshape(bn, seq, d), kbufs[g].reshape(n_odd, bn, seq, d),
                               vbufs[g].reshape(n_odd, bn, seq, d))
                mixed.append(o.reshape(rows[g], d))
            w_mix = p['w_o']
        xs, hs = matmul_post_norm(mixed, w_mix, i, xs, p['norm_mix_post'][l], p['norm_ffn_pre'][l], bm=512)
        a = swiglu_up(hs, p['w_gate'], p['w_up'], l, bm=bm, bn=512)
        ms = matmul(a, p['w_down'], l, bm=512, bn=512, out_dtypes=(BF16, BF16))
        g_next = p['norm_mix_pre'][l + 1] if l + 1 < DEPTH else None
        xs, hs = zip(*[post_norm(xs[g], ms[g], p['norm_ffn_post'][l], g_next) for g in groups])
    out = []
    for g in groups:
        bn, seq, _ = shapes[g]
        kv_shape = (n_odd, bn, seq, SB_HEADS, SB_HEAD_DIM)
        out.append((xs[g].reshape(bn, seq, d), jnp.stack(conv_new[g]), jnp.stack(sgv_new[g]),
                    kbufs[g].reshape(kv_shape), vbufs[g].reshape(kv_shape)))
    return out


def kernel(x_prompt, x_sample, state_conv, cache_k, cache_v, page_table, norm_mix_pre, norm_mix_post,
           norm_ffn_pre, norm_ffn_post, w_in_ab, conv_w, sg_ln_g, sg_ln_b, sg_w, sg_b, w_out_ab, w_qkv,
           sb_bias, w_o, w_gate, w_up, w_down):
    p = dict(norm_mix_pre=norm_mix_pre, norm_mix_post=norm_mix_post, norm_ffn_pre=norm_ffn_pre,
             norm_ffn_post=norm_ffn_post, w_in_ab=w_in_ab, conv_w=conv_w, sg_ln_g=sg_ln_g,
             sg_ln_b=sg_ln_b, sg_w=sg_w, sg_b=sg_b, w_out_ab=w_out_ab, w_qkv=w_qkv, sb_bias=sb_bias,
             w_o=w_o, w_gate=w_gate, w_up=w_up, w_down=w_down)
    n_even = state_conv.shape[0]

    bp = x_prompt.shape[0]
    conv0 = jnp.zeros((n_even, bp, CONV_W - 1, CONV_DIM), x_prompt.dtype)

    def attend_prompt(i, q, kbuf, vbuf):
        return sb_attn_prompt(q, kbuf, vbuf, i, sb_bias[i], tq=512, tk=256)

    def attend_sample(i, q, kbuf, vbuf):
        return sb_attn_sample(q, kbuf, vbuf, i, cache_k, cache_v, page_table, sb_bias[i])

    (y_prompt, conv_p, sgv_p, k_p, v_p), (y_sample, conv_s, sgv_s, k_s, v_s) = _trunks(
        (x_prompt, x_sample), (conv0, state_conv), (attend_prompt, attend_sample), p, bm=1024, even_rows=256)
    return (y_prompt, y_sample, conv_p, conv_s, sgv_p, sgv_s, k_p, v_p, k_s, v_s)
```

```python
import functools

import jax
import jax.numpy as jnp
from jax import lax
from jax.experimental import pallas as pl
from jax.experimental.pallas import tpu as pltpu

D_MODEL = 2048
DEPTH = 4
PAGE_SIZE = 128
CONV_DIM = D_MODEL // 2
CONV_W = 3
SG_DIM = D_MODEL // 2
SG_HEADS = 8
SG_HEAD_DIM = SG_DIM // SG_HEADS
CHUNK = 128
SB_HEADS = 16
SB_HEAD_DIM = D_MODEL // SB_HEADS
D_FF = 5632
IN_AB = 3 * CONV_DIM + 2 * SG_DIM
EPS = 1e-6

V7X_LANES = 128
V7X_SUBLANES = 8
V7X_VMEM_LIMIT_BYTES = 56 * 1024 * 1024

BF16 = jnp.bfloat16
F32 = jnp.float32

HEADS_PER_STEP = 4
SAMPLE_PAGES_PER_STEP = 8
KEYS_PER_TILE = V7X_LANES // SB_HEADS
NT_DIMS = (((1,), (1,)), ((), ()))


def _cparams(n_axes):
    return pltpu.CompilerParams(
        dimension_semantics=("arbitrary",) * n_axes,
        vmem_limit_bytes=V7X_VMEM_LIMIT_BYTES,
    )


def _row_block(m, want):
    return want if m % want == 0 else m


def _rms_cast_kernel(x_ref, g_ref, h_ref):
    x = x_ref[...]
    y = x * lax.rsqrt(jnp.mean(x * x, axis=-1, keepdims=True) + EPS)
    h_ref[...] = (y * g_ref[...]).astype(h_ref.dtype)


def rms_cast(x, g):
    m, d = x.shape
    bm = _row_block(m, 512)
    return pl.pallas_call(
        _rms_cast_kernel,
        out_shape=jax.ShapeDtypeStruct((m, d), BF16),
        grid=(m // bm,),
        in_specs=[pl.BlockSpec((bm, d), lambda i: (i, 0)),
                  pl.BlockSpec((1, d), lambda i: (0, 0))],
        out_specs=pl.BlockSpec((bm, d), lambda i: (i, 0)),
        compiler_params=_cparams(1),
        name="rms_cast",
    )(x, g.reshape(1, d))


def _post_norm_kernel(x_ref, m_ref, gp_ref, gn_ref, xo_ref, *maybe_h_ref):
    m = m_ref[...].astype(F32)
    y = m * lax.rsqrt(jnp.mean(m * m, axis=-1, keepdims=True) + EPS)
    xn = x_ref[...] + y * gp_ref[...]
    xo_ref[...] = xn
    for h_ref in maybe_h_ref:
        hn = xn * lax.rsqrt(jnp.mean(xn * xn, axis=-1, keepdims=True) + EPS)
        h_ref[...] = (hn * gn_ref[...]).astype(h_ref.dtype)


def post_norm(x, m, g_post, g_next):
    rows, d = x.shape
    bm = _row_block(rows, 512)
    with_next = g_next is not None
    row_spec = pl.BlockSpec((bm, d), lambda i: (i, 0))
    gain_spec = pl.BlockSpec((1, d), lambda i: (0, 0))
    out_shape = [jax.ShapeDtypeStruct((rows, d), F32)]
    if with_next:
        out_shape.append(jax.ShapeDtypeStruct((rows, d), BF16))
    else:
        g_next = g_post
    out = pl.pallas_call(
        _post_norm_kernel,
        out_shape=tuple(out_shape),
        grid=(rows // bm,),
        in_specs=[row_spec, row_spec, gain_spec, gain_spec],
        out_specs=tuple([row_spec] * len(out_shape)),
        compiler_params=_cparams(1),
        name="post_norm",
    )(x, m, g_post.reshape(1, d), g_next.reshape(1, d))
    return (out[0], out[1]) if with_next else (out[0], None)


def _on_group(row_step, fn, main_refs, small_refs):
    pl.when(row_step == 0)(lambda: fn(*small_refs))
    pl.when(row_step > 0)(lambda: fn(*main_refs))


def _main_row(row_step):
    return jnp.maximum(row_step - 1, 0)


def _mm_kernel(x1_ref, x2_ref, w_ref, *rest):
    o1_ref, o2_ref, wb_ref = rest[-3:]
    i = pl.program_id(1)

    @pl.when(i == 0)
    def _():
        wb_ref[...] = w_ref[...].astype(BF16)

    def product(x_ref, o_ref):
        o_ref[...] = jnp.dot(x_ref[...].astype(BF16), wb_ref[...],
                             preferred_element_type=F32).astype(o_ref.dtype)

    _on_group(i, product, (x1_ref, o1_ref), (x2_ref, o2_ref))


def matmul(xs, w, layer, *, bm, bn, out_dtypes=(F32, F32), col0=0, n_cols=None, into=None):
    (m1, k), (m2, _) = xs[0].shape, xs[1].shape
    n_cols = w.shape[2] if n_cols is None else n_cols
    n_main = m1 // bm
    jb0 = col0 // bn
    main_row = _main_row
    in_specs = [pl.BlockSpec((bm, k), lambda j, i: (main_row(i), 0)),
                pl.BlockSpec((m2, k), lambda j, i: (0, 0)),
                pl.BlockSpec((None, k, bn), lambda j, i: (layer, 0, jb0 + j))]
    args = [xs[0], xs[1], w]
    if into is None:
        out_shape = (jax.ShapeDtypeStruct((m1, n_cols), out_dtypes[0]),
                     jax.ShapeDtypeStruct((m2, n_cols), out_dtypes[1]))
        out_specs = (pl.BlockSpec((bm, bn), lambda j, i: (main_row(i), j)),
                     pl.BlockSpec((m2, bn), lambda j, i: (0, j)))
        aliases = {}
    else:
        bufs, slot = into
        out_shape = tuple(jax.ShapeDtypeStruct(b.shape, b.dtype) for b in bufs)
        out_specs = (pl.BlockSpec((None, bm, bn), lambda j, i: (slot, main_row(i), j)),
                     pl.BlockSpec((None, m2, bn), lambda j, i: (slot, 0, j)))
        aliases = {}
        if not isinstance(bufs[0], jax.ShapeDtypeStruct):
            in_specs += [pl.BlockSpec(memory_space=pl.ANY)] * 2
            args += list(bufs)
            aliases = {3: 0, 4: 1}
    return pl.pallas_call(
        _mm_kernel,
        out_shape=out_shape,
        grid=(n_cols // bn, n_main + 1),
        in_specs=in_specs,
        out_specs=out_specs,
        scratch_shapes=[pltpu.VMEM((k, bn), BF16)],
        input_output_aliases=aliases,
        compiler_params=_cparams(2),
        name="matmul",
    )(*args)


def _mm_post_norm_kernel(y1_ref, y2_ref, w_ref, x1_ref, x2_ref, gp_ref, gn_ref,
                         xo1_ref, h1_ref, xo2_ref, h2_ref, wb_ref):
    i = pl.program_id(0)

    @pl.when(i == 0)
    def _():
        wb_ref[...] = w_ref[...].astype(BF16)

    def project(y_ref, x_ref, xo_ref, h_ref):
        m = jnp.dot(y_ref[...].astype(BF16), wb_ref[...], preferred_element_type=F32)
        nm = m * lax.rsqrt(jnp.mean(m * m, axis=-1, keepdims=True) + EPS)
        xn = x_ref[...] + nm * gp_ref[...]
        xo_ref[...] = xn
        hn = xn * lax.rsqrt(jnp.mean(xn * xn, axis=-1, keepdims=True) + EPS)
        h_ref[...] = (hn * gn_ref[...]).astype(h_ref.dtype)

    _on_group(i, project, (y1_ref, x1_ref, xo1_ref, h1_ref), (y2_ref, x2_ref, xo2_ref, h2_ref))


def matmul_post_norm(ys, w, layer, xs, g_post, g_next, *, bm):
    (m1, k), (m2, _) = ys[0].shape, ys[1].shape
    n = w.shape[2]
    n_main = m1 // bm
    main_in = pl.BlockSpec((bm, k), lambda i: (_main_row(i), 0))
    main_row_spec = pl.BlockSpec((bm, n), lambda i: (_main_row(i), 0))
    tail_in = pl.BlockSpec((m2, k), lambda i: (0, 0))
    tail_row_spec = pl.BlockSpec((m2, n), lambda i: (0, 0))
    gain_spec = pl.BlockSpec((1, n), lambda i: (0, 0))
    xo1, h1, xo2, h2 = pl.pallas_call(
        _mm_post_norm_kernel,
        out_shape=(jax.ShapeDtypeStruct((m1, n), F32), jax.ShapeDtypeStruct((m1, n), BF16),
                   jax.ShapeDtypeStruct((m2, n), F32), jax.ShapeDtypeStruct((m2, n), BF16)),
        grid=(n_main + 1,),
        in_specs=[main_in, tail_in,
                  pl.BlockSpec((None, k, n), lambda i: (layer, 0, 0), pipeline_mode=pl.Buffered(1)),
                  main_row_spec, tail_row_spec, gain_spec, gain_spec],
        out_specs=(main_row_spec, main_row_spec, tail_row_spec, tail_row_spec),
        scratch_shapes=[pltpu.VMEM((k, n), BF16)],
        compiler_params=_cparams(1),
        name="matmul_post_norm",
    )(ys[0], ys[1], w, xs[0], xs[1], g_post.reshape(1, n), g_next.reshape(1, n))
    return (xo1, xo2), (h1, h2)


def _swiglu_up_kernel(x1_ref, x2_ref, wg_ref, wu_ref, o1_ref, o2_ref, wgb_ref, wub_ref):
    i = pl.program_id(1)

    @pl.when(i == 0)
    def _():
        wgb_ref[...] = wg_ref[...].astype(BF16)
        wub_ref[...] = wu_ref[...].astype(BF16)

    def gated(x_ref, o_ref):
        x = x_ref[...]
        g = jnp.dot(x, wgb_ref[...], preferred_element_type=F32)
        u = jnp.dot(x, wub_ref[...], preferred_element_type=F32)
        o_ref[...] = (g * jax.nn.sigmoid(g) * u).astype(o_ref.dtype)

    _on_group(i, gated, (x1_ref, o1_ref), (x2_ref, o2_ref))


def swiglu_up(xs, wg, wu, layer, *, bm, bn):
    (m1, k), (m2, _) = xs[0].shape, xs[1].shape
    n = wg.shape[2]
    n_main = m1 // bm
    main_row = _main_row
    w_spec = pl.BlockSpec((None, k, bn), lambda j, i: (layer, 0, j))
    return pl.pallas_call(
        _swiglu_up_kernel,
        out_shape=(jax.ShapeDtypeStruct((m1, n), BF16), jax.ShapeDtypeStruct((m2, n), BF16)),
        grid=(n // bn, n_main + 1),
        in_specs=[pl.BlockSpec((bm, k), lambda j, i: (main_row(i), 0)),
                  pl.BlockSpec((m2, k), lambda j, i: (0, 0)),
                  w_spec, w_spec],
        out_specs=(pl.BlockSpec((bm, bn), lambda j, i: (main_row(i), j)),
                   pl.BlockSpec((m2, bn), lambda j, i: (0, j))),
        scratch_shapes=[pltpu.VMEM((k, bn), BF16), pltpu.VMEM((k, bn), BF16)],
        compiler_params=_cparams(2),
        name="swiglu_up",
    )(xs[0], xs[1], wg, wu)


def _even_mix_kernel(gb_ref, gc_ref, xa_ref, u_ref, vv_ref, prefix_ref, cw_ref, lng_ref, lnb_ref,
                     sgw_ref, sgbt_ref, y_ref, cnew_ref, vlast_ref, cbuf_ref, wtri_ref, pad_ref,
                     *, rows, n_steps):
    b = pl.program_id(0)
    l = pl.program_id(1)
    carry_at = V7X_SUBLANES - (CONV_W - 1)

    @pl.when((b == 0) & (l == 0))
    def _():
        t = lax.broadcasted_iota(jnp.int32, (CHUNK, CHUNK), 0)
        s = lax.broadcasted_iota(jnp.int32, (CHUNK, CHUNK), 1)
        for h in range(SG_HEADS):
            wtri_ref[h] = jnp.where(s <= t, sgw_ref[h], 0.0).astype(BF16)

    @pl.when(l == 0)
    def _():
        cbuf_ref[carry_at:V7X_SUBLANES, :] = prefix_ref[0]

    c = gc_ref[0].astype(F32) * xa_ref[0].astype(F32)
    cbuf_ref[V7X_SUBLANES:V7X_SUBLANES + rows, :] = c
    conv = cw_ref[CONV_W - 1:CONV_W, :] * c
    for j in range(CONV_W - 1):
        conv = conv + cw_ref[j:j + 1, :] * cbuf_ref[carry_at + j:carry_at + j + rows, :]
    y_ref[0, :, 0:CONV_DIM] = (gb_ref[0].astype(F32) * conv).astype(y_ref.dtype)
    new_hist = cbuf_ref[V7X_SUBLANES + rows - (CONV_W - 1):V7X_SUBLANES + rows, :]
    cbuf_ref[carry_at:V7X_SUBLANES, :] = new_hist

    @pl.when(l == n_steps - 1)
    def _():
        cnew_ref[0] = new_hist

    vv = vv_ref[0].astype(F32)
    mu = jnp.mean(vv, axis=-1, keepdims=True)
    xc = vv - mu
    vn = xc * lax.rsqrt(jnp.mean(xc * xc, axis=-1, keepdims=True) + EPS) * lng_ref[...] + lnb_ref[...]

    n_chunks = -(-rows // CHUNK)
    last_rows = rows - (n_chunks - 1) * CHUNK

    @pl.when(l == n_steps - 1)
    def _():
        vlast_ref[0] = vn[(n_chunks - 1) * CHUNK:, :]

    if rows % CHUNK != 0:
        pad_ref[...] = jnp.zeros_like(pad_ref)
        pad_ref[0:rows, :] = vn
        vn_full = pad_ref[...]
    else:
        vn_full = vn
    vnb = vn_full.astype(BF16)
    u = u_ref[0].astype(F32)
    for ci in range(n_chunks):
        r0 = ci * CHUNK
        nr = CHUNK if ci < n_chunks - 1 else last_rows
        for h in range(SG_HEADS):
            c0 = h * SG_HEAD_DIM
            s = jnp.dot(wtri_ref[h], vnb[r0:r0 + CHUNK, c0:c0 + SG_HEAD_DIM], preferred_element_type=F32)
            s = s + sgbt_ref[:, h:h + 1]
            yb = u[r0:r0 + nr, c0:c0 + SG_HEAD_DIM] * s[0:nr]
            y_ref[0, r0:r0 + nr, CONV_DIM + c0:CONV_DIM + c0 + SG_HEAD_DIM] = yb.astype(y_ref.dtype)


def even_mix(proj, prefix, conv_w, ln_g, ln_b, sg_w, sg_b, *, rows):
    bn, seq, _ = proj.shape
    rows = min(rows, seq)
    n_steps = seq // rows
    r_last = seq - ((seq - 1) // CHUNK) * CHUNK
    pad_rows = -(-rows // CHUNK) * CHUNK

    def col(cb):
        return pl.BlockSpec((1, rows, CONV_DIM), lambda b, l: (b, l, cb))

    const2 = lambda b, l: (0, 0)
    y_dtype = BF16 if rows % 16 == 0 else F32
    return pl.pallas_call(
        functools.partial(_even_mix_kernel, rows=rows, n_steps=n_steps),
        out_shape=(jax.ShapeDtypeStruct((bn, seq, D_MODEL), y_dtype),
                   jax.ShapeDtypeStruct((bn, CONV_W - 1, CONV_DIM), F32),
                   jax.ShapeDtypeStruct((bn, r_last, SG_DIM), F32)),
        grid=(bn, n_steps),
        in_specs=[col(0), col(1), col(2), col(3), col(4),
                  pl.BlockSpec((1, CONV_W - 1, CONV_DIM), lambda b, l: (b, 0, 0)),
                  pl.BlockSpec((CONV_W, CONV_DIM), const2),
                  pl.BlockSpec((1, SG_DIM), const2),
                  pl.BlockSpec((1, SG_DIM), const2),
                  pl.BlockSpec((SG_HEADS, CHUNK, CHUNK), lambda b, l: (0, 0, 0)),
                  pl.BlockSpec((CHUNK, SG_HEADS), const2)],
        out_specs=(pl.BlockSpec((1, rows, D_MODEL), lambda b, l: (b, l, 0)),
                   pl.BlockSpec((1, CONV_W - 1, CONV_DIM), lambda b, l: (b, 0, 0)),
                   pl.BlockSpec((1, r_last, SG_DIM), lambda b, l: (b, 0, 0))),
        scratch_shapes=[pltpu.VMEM((V7X_SUBLANES + rows, CONV_DIM), F32),
                        pltpu.VMEM((SG_HEADS, CHUNK, CHUNK), BF16),
                        pltpu.VMEM((pad_rows, SG_DIM), F32)],
        compiler_params=_cparams(2),
        name="even_mix",
    )(proj, proj, proj, proj, proj, prefix, conv_w, ln_g.reshape(1, SG_DIM), ln_b.reshape(1, SG_DIM),
      sg_w, sg_b.T)


def _softplus_terms(z):
    sp = jnp.maximum(z, 0.0) + jnp.log(1.0 + jnp.exp(-jnp.abs(z)))
    return sp, z - sp


def _split_bf16(x):
    hi = x.astype(BF16)
    lo = (x - hi.astype(F32)).astype(BF16)
    return hi, lo


def _sb_prompt_kernel(bias_ref, q_ref, k_ref, v_ref, o_ref, kb_ref, vb_ref, later_ref, *, tq, tk):
    b = pl.program_id(0)
    hp = pl.program_id(1)
    i = pl.program_id(2)
    blocks_per_tile = tq // tk

    @pl.when((b == 0) & (hp == 0) & (i == 0))
    def _():
        row = lax.broadcasted_iota(jnp.int32, (tk, tk), 0)
        col = lax.broadcasted_iota(jnp.int32, (tk, tk), 1)
        later = jnp.where(row > col, 1.0, 0.0).astype(BF16)
        later_ref[0:tk, :] = later
        later_ref[tk:2 * tk, :] = later

    @pl.when(i == 0)
    def _():
        kb_ref[...] = k_ref[0].astype(BF16)
        vb_ref[...] = v_ref[0].astype(BF16)

    scale = SB_HEAD_DIM ** -0.5
    q = q_ref[0].astype(BF16)

    heads = range(HEADS_PER_STEP)

    def block(j, carry, first_row):
        lanes = [slice(g * SB_HEAD_DIM, (g + 1) * SB_HEAD_DIM) for g in heads]
        start = pl.multiple_of(j * tk, tk)
        r0 = 0 if first_row is None else first_row
        acc = [carry[2 * g][r0:] for g in heads]
        run = [carry[2 * g + 1][r0:] for g in heads]
        z = [lax.dot_general(q[r0:, lanes[g]], kb_ref[pl.ds(start, tk), lanes[g]], NT_DIMS,
                             preferred_element_type=F32) * scale + bias_ref[hp * HEADS_PER_STEP + g]
             for g in heads]
        terms = [_softplus_terms(z[g]) for g in heads]
        sp = [terms[g][0] for g in heads]
        log_beta = [terms[g][1] for g in heads]
        if first_row is not None:
            row = lax.broadcasted_iota(jnp.int32, (tq - r0, tk), 0)
            col = lax.broadcasted_iota(jnp.int32, (tq - r0, tk), 1)
            visible = col < row
            sp = [jnp.where(visible, sp[g], 0.0) for g in heads]
        split = [_split_bf16(sp[g]) for g in heads]
        behind = [jnp.dot(jnp.concatenate(split[g], axis=1), later_ref[...], preferred_element_type=F32)
                  for g in heads]
        p = [jnp.exp(log_beta[g] - behind[g] - run[g]) for g in heads]
        if first_row is not None:
            p = [jnp.where(visible, p[g], 0.0) for g in heads]
        out = []
        for g in heads:
            vs = vb_ref[pl.ds(start, tk), lanes[g]]
            acc_g = acc[g] + jnp.dot(p[g].astype(BF16), vs, preferred_element_type=F32)
            run_g = run[g] + (behind[g][:, 0:1] + sp[g][:, 0:1])
            if r0:
                acc_g = jnp.concatenate([carry[2 * g][:r0], acc_g], axis=0)
                run_g = jnp.concatenate([carry[2 * g + 1][:r0], run_g], axis=0)
            out.extend((acc_g, run_g))
        return tuple(out)

    carry = tuple(c for _ in heads for c in (jnp.zeros((tq, SB_HEAD_DIM), F32), jnp.zeros((tq, 1), F32)))
    for d in reversed(range(blocks_per_tile)):
        carry = block(i * blocks_per_tile + d, carry, d * tk)
    carry = lax.fori_loop(0, i * blocks_per_tile,
                          lambda step, c: block(i * blocks_per_tile - 1 - step, c, None), carry)
    for g in heads:
        o_ref[0, :, g * SB_HEAD_DIM:(g + 1) * SB_HEAD_DIM] = carry[2 * g].astype(o_ref.dtype)


def sb_attn_prompt(q, kbuf, vbuf, layer, bias, *, tq, tk):
    bn, seq, _ = q.shape
    nq = seq // tq
    width = HEADS_PER_STEP * SB_HEAD_DIM
    kv_spec = pl.BlockSpec((None, 1, seq, width), lambda b, hp, i: (layer, b, 0, hp))
    return pl.pallas_call(
        functools.partial(_sb_prompt_kernel, tq=tq, tk=tk),
        out_shape=jax.ShapeDtypeStruct((bn, seq, D_MODEL), BF16),
        grid=(bn, SB_HEADS // HEADS_PER_STEP, nq),
        in_specs=[pl.BlockSpec(memory_space=pltpu.SMEM),
                  pl.BlockSpec((1, tq, width), lambda b, hp, i: (b, i, hp)),
                  kv_spec, kv_spec],
        out_specs=pl.BlockSpec((1, tq, width), lambda b, hp, i: (b, i, hp)),
        scratch_shapes=[pltpu.VMEM((seq, width), BF16), pltpu.VMEM((seq, width), BF16),
                        pltpu.VMEM((2 * tk, tk), BF16)],
        compiler_params=_cparams(3),
        name="sb_attn_prompt",
    )(bias, q, kbuf, vbuf)


def _head_of(pair_index):
    return pair_index & (SB_HEADS - 1)


def _key_of(pair_index):
    return pair_index >> (SB_HEADS.bit_length() - 1)


def _sb_sample_kernel(pt_ref, q_ref, bias_ref, knew_ref, vnew_ref, *rest, n_new, pages_per_step):
    kpg_refs = rest[0:pages_per_step]
    vpg_refs = rest[pages_per_step:2 * pages_per_step]
    o_ref, acc_ref, run_ref, new_ref, scanw_ref = rest[2 * pages_per_step:]
    b = pl.program_id(0)
    s = pl.program_id(1)
    n_steps = pl.num_programs(1)
    scale = SB_HEAD_DIM ** -0.5
    n_rows = n_new * SB_HEADS

    @pl.when((b == 0) & (s == 0))
    def _():
        r = lax.broadcasted_iota(jnp.int32, (V7X_LANES, 2 * V7X_LANES), 0)
        c = lax.broadcasted_iota(jnp.int32, (V7X_LANES, 2 * V7X_LANES), 1)
        same_head = _head_of(r) == _head_of(c)
        after = _key_of(r) > _key_of(c & (V7X_LANES - 1))
        w = jnp.where(same_head & (after | (c >= V7X_LANES)), 1.0, 0.0).astype(BF16)
        scanw_ref[0:V7X_LANES, :] = w
        scanw_ref[V7X_LANES:2 * V7X_LANES, :] = w

    head_row = lax.broadcasted_iota(jnp.int32, (SB_HEADS, V7X_LANES), 0)
    head_lane = _head_of(lax.broadcasted_iota(jnp.int32, (SB_HEADS, V7X_LANES), 1))
    own = head_row == head_lane
    query_row = lax.broadcasted_iota(jnp.int32, (V7X_SUBLANES, V7X_LANES), 0)

    def absorb(k2d, v2d, n_tiles, visible, acc, run):
        zfull = lax.dot_general(q_ref[0], k2d, NT_DIMS, preferred_element_type=F32)
        tiles = []
        for t in range(n_tiles):
            lanes = slice(t * V7X_LANES, (t + 1) * V7X_LANES)
            tile = jnp.zeros((V7X_SUBLANES, V7X_LANES), F32)
            for i in range(n_new):
                own_head = jnp.sum(jnp.where(own, zfull[i * SB_HEADS:(i + 1) * SB_HEADS, lanes], 0.0),
                                   axis=0, keepdims=True)
                tile = jnp.where(query_row == i, own_head, tile)
            tiles.append(tile)
        z = jnp.concatenate(tiles, axis=0) * scale + bias_ref[...]
        sp, log_beta = _softplus_terms(z)
        if visible is not None:
            sp = jnp.where(visible, sp, 0.0)
        hi, lo = _split_bf16(sp)
        scan = jnp.dot(jnp.concatenate([hi, lo], axis=1), scanw_ref[...], preferred_element_type=F32)
        p_tiles = [None] * n_tiles
        for t in reversed(range(n_tiles)):
            rs = slice(t * V7X_SUBLANES, (t + 1) * V7X_SUBLANES)
            p_t = jnp.exp(log_beta[rs] - scan[rs, 0:V7X_LANES] - run)
            if visible is not None:
                p_t = jnp.where(visible[rs], p_t, 0.0)
            p_tiles[t] = p_t
            run = run + scan[rs, V7X_LANES:2 * V7X_LANES]
        blocks = []
        for i in range(n_new):
            blocks.append(jnp.concatenate(
                [jnp.where(own, jnp.broadcast_to(p_tiles[t][i:i + 1, :], (SB_HEADS, V7X_LANES)), 0.0)
                 for t in range(n_tiles)], axis=1).astype(BF16))
        pbd = jnp.concatenate(blocks, axis=0)
        return acc + jnp.dot(pbd, v2d, preferred_element_type=F32), run

    @pl.when(s == 0)
    def _():
        new_ref[...] = jnp.zeros_like(new_ref)
        new_ref[0, 0:n_new] = knew_ref[0]
        new_ref[1, 0:n_new] = vnew_ref[0]
        key = _key_of(lax.broadcasted_iota(jnp.int32, (V7X_SUBLANES, V7X_LANES), 1))
        acc, run = absorb(new_ref[0].reshape(V7X_LANES, SB_HEAD_DIM).astype(BF16),
                          new_ref[1].reshape(V7X_LANES, SB_HEAD_DIM).astype(BF16), 1, key < query_row,
                          jnp.zeros(acc_ref.shape, F32), jnp.zeros(run_ref.shape, F32))
        acc_ref[...] = acc
        run_ref[...] = run

    @pl.when(s > 0)
    def _():
        n_rows_page = PAGE_SIZE * SB_HEADS
        acc, run = acc_ref[...], run_ref[...]
        for u in reversed(range(pages_per_step)):
            acc, run = absorb(kpg_refs[u][...].reshape(n_rows_page, SB_HEAD_DIM).astype(BF16),
                              vpg_refs[u][...].reshape(n_rows_page, SB_HEAD_DIM).astype(BF16),
                              n_rows_page // V7X_LANES, None, acc, run)
        acc_ref[...] = acc
        run_ref[...] = run

    @pl.when(s == n_steps - 1)
    def _():
        for i in range(n_new):
            for h in range(SB_HEADS):
                r = i * SB_HEADS + h
                o_ref[0, i:i + 1, h * SB_HEAD_DIM:(h + 1) * SB_HEAD_DIM] = acc_ref[r:r + 1, :]


def sb_attn_sample(q, kbuf, vbuf, layer, cache_k, cache_v, page_table, bias):
    bn, n_new, _ = q.shape
    assert n_new <= V7X_SUBLANES and KEYS_PER_TILE >= n_new
    n_layers = kbuf.shape[0]
    n_pages = page_table.shape[1]
    n_rows = n_new * SB_HEADS
    q_rows = q.reshape(bn, n_rows, SB_HEAD_DIM).astype(BF16)
    bias_l = jnp.tile(bias, KEYS_PER_TILE).reshape(1, V7X_LANES)
    k_new = kbuf.reshape(n_layers, bn, n_new, SB_HEADS, SB_HEAD_DIM)
    v_new = vbuf.reshape(n_layers, bn, n_new, SB_HEADS, SB_HEAD_DIM)

    pps = next(c for c in (SAMPLE_PAGES_PER_STEP, 2, 1) if n_pages % c == 0)

    def page_spec(u):
        def index(b, s, pt):
            return (layer, pt[b, n_pages - jnp.maximum(s, 1) * pps + u], 0, 0, 0)
        return pl.BlockSpec((None, None, PAGE_SIZE, SB_HEADS, SB_HEAD_DIM), index)

    new_spec = pl.BlockSpec((None, 1, n_new, SB_HEADS, SB_HEAD_DIM), lambda b, s, pt: (layer, b, 0, 0, 0))
    page_specs = [page_spec(u) for u in range(pps)]
    grid_spec = pltpu.PrefetchScalarGridSpec(
        num_scalar_prefetch=1,
        grid=(bn, n_pages // pps + 1),
        in_specs=[pl.BlockSpec((1, n_rows, SB_HEAD_DIM), lambda b, s, pt: (b, 0, 0)),
                  pl.BlockSpec((1, V7X_LANES), lambda b, s, pt: (0, 0)),
                  new_spec, new_spec] + page_specs + page_specs,
        out_specs=pl.BlockSpec((1, n_new, D_MODEL), lambda b, s, pt: (b, 0, 0)),
        scratch_shapes=[pltpu.VMEM((n_rows, SB_HEAD_DIM), F32),
                        pltpu.VMEM((V7X_SUBLANES, V7X_LANES), F32),
                        pltpu.VMEM((2, KEYS_PER_TILE, SB_HEADS, SB_HEAD_DIM), F32),
                        pltpu.VMEM((2 * V7X_LANES, 2 * V7X_LANES), BF16)],
    )
    return pl.pallas_call(
        functools.partial(_sb_sample_kernel, n_new=n_new, pages_per_step=pps),
        out_shape=jax.ShapeDtypeStruct((bn, n_new, D_MODEL), F32),
        grid_spec=grid_spec,
        compiler_params=_cparams(2),
        name="sb_attn_sample",
    )(page_table, q_rows, bias_l, k_new, v_new, *([cache_k] * pps), *([cache_v] * pps))


def _trunks(x3s, conv_states, attends, p, *, bm, even_rows):
    groups = range(len(x3s))
    shapes = [x3.shape for x3 in x3s]
    d = D_MODEL
    rows = [bn * seq for bn, seq, _ in shapes]
    n_odd = p['w_qkv'].shape[0]
    xs = [x3.reshape(rows[g], d) for g, x3 in enumerate(x3s)]
    hs = [rms_cast(x, p['norm_mix_pre'][0]) for x in xs]
    conv_new = [[] for _ in groups]
    sgv_new = [[] for _ in groups]
    kbufs = tuple(jax.ShapeDtypeStruct((n_odd, rows[g], d), F32) for g in groups)
    vbufs = kbufs
    for l in range(DEPTH):
        i = l // 2
        if l % 2 == 0:
            proj_dtypes = tuple(BF16 if seq % 16 == 0 else F32 for _, seq, _ in shapes)
            projs = matmul(hs, p['w_in_ab'], i, bm=bm, bn=1024, out_dtypes=proj_dtypes)
            mixed = []
            for g in groups:
                bn, seq, _ = shapes[g]
                y, c, vr = even_mix(projs[g].reshape(bn, seq, IN_AB), conv_states[g][i], p['conv_w'][i],
                                    p['sg_ln_g'][i], p['sg_ln_b'][i], p['sg_w'][i], p['sg_b'][i],
                                    rows=even_rows)
                conv_new[g].append(c)
                sgv_new[g].append(vr)
                mixed.append(y.reshape(rows[g], d))
            w_mix = p['w_out_ab']
        else:
            qs = matmul(hs, p['w_qkv'], i, bm=bm, bn=1024, col0=0, n_cols=d, out_dtypes=(BF16, BF16))
            kbufs = matmul(hs, p['w_qkv'], i, bm=bm, bn=1024, col0=d, n_cols=d, into=(kbufs, i))
            vbufs = matmul(hs, p['w_qkv'], i, bm=bm, bn=1024, col0=2 * d, n_cols=d, into=(vbufs, i))
            mixed = []
            for g in groups:
                bn, seq, _ = shapes[g]
                o = attends[g](i, qs[g].reshape(bn, seq, d), kbufs[g].reshape(n_odd, bn, seq, d),
                               vbufs[g].reshape(n_odd, bn, seq, d))
                mixed.append(o.reshape(rows[g], d))
            w_mix = p['w_o']
        xs, hs = matmul_post_norm(mixed, w_mix, i, xs, p['norm_mix_post'][l], p['norm_ffn_pre'][l], bm=512)
        a = swiglu_up(hs, p['w_gate'], p['w_up'], l, bm=bm, bn=512)
        ms = matmul(a, p['w_down'], l, bm=512, bn=512, out_dtypes=(BF16, BF16))
        g_next = p['norm_mix_pre'][l + 1] if l + 1 < DEPTH else None
        xs, hs = zip(*[post_norm(xs[g], ms[g], p['norm_ffn_post'][l], g_next) for g in groups])
    out = []
    for g in groups:
        bn, seq, _ = shapes[g]
        kv_shape = (n_odd, bn, seq, SB_HEADS, SB_HEAD_DIM)
        out.append((xs[g].reshape(bn, seq, d), jnp.stack(conv_new[g]), jnp.stack(sgv_new[g]),
                    kbufs[g].reshape(kv_shape), vbufs[g].reshape(kv_shape)))
    return out


def kernel(x_prompt, x_sample, state_conv, cache_k, cache_v, page_table, norm_mix_pre, norm_mix_post,
           norm_ffn_pre, norm_ffn_post, w_in_ab, conv_w, sg_ln_g, sg_ln_b, sg_w, sg_b, w_out_ab, w_qkv,
           sb_bias, w_o, w_gate, w_up, w_down):
    p = dict(norm_mix_pre=norm_mix_pre, norm_mix_post=norm_mix_post, norm_ffn_pre=norm_ffn_pre,
             norm_ffn_post=norm_ffn_post, w_in_ab=w_in_ab, conv_w=conv_w, sg_ln_g=sg_ln_g,
             sg_ln_b=sg_ln_b, sg_w=sg_w, sg_b=sg_b, w_out_ab=w_out_ab, w_qkv=w_qkv, sb_bias=sb_bias,
             w_o=w_o, w_gate=w_gate, w_up=w_up, w_down=w_down)
    n_even = state_conv.shape[0]

    bp = x_prompt.shape[0]
    conv0 = jnp.zeros((n_even, bp, CONV_W - 1, CONV_DIM), x_prompt.dtype)

    def attend_prompt(i, q, kbuf, vbuf):
        return sb_attn_prompt(q, kbuf, vbuf, i, sb_bias[i], tq=512, tk=256)

    def attend_sample(i, q, kbuf, vbuf):
        return sb_attn_sample(q, kbuf, vbuf, i, cache_k, cache_v, page_table, sb_bias[i])

    (y_prompt, conv_p, sgv_p, k_p, v_p), (y_sample, conv_s, sgv_s, k_s, v_s) = _trunks(
        (x_prompt, x_sample), (conv0, state_conv), (attend_prompt, attend_sample), p, bm=1024, even_rows=256)
    return (y_prompt, y_sample, conv_p, conv_s, sgv_p, sgv_s, k_p, v_p, k_s, v_s)
```
